```python
import math
import jax, jax.numpy as jnp
from jax import lax
import numpy as np

D_MODEL = 4096
BATCH = 2
SEQ = 8192
DEPTH = 1

GRID_W = 64
CTX_LEN = 256
D_MIX = D_MODEL
FNET_WIDTH = D_MIX // 4
FNET_GROUPS = 4
FNET_GROUP_DIM = FNET_WIDTH // FNET_GROUPS
SSD_WIDTH = D_MIX - FNET_WIDTH
SSD_HEAD_DIM = 64
SSD_HEADS = SSD_WIDTH // SSD_HEAD_DIM
SSD_GROUPS = 8
SSD_HPG = SSD_HEADS // SSD_GROUPS
SSD_STATE = 128
SSD_GN = SSD_GROUPS * SSD_STATE
SSD_CONV = 5
SSD_CHUNK = 128
SSD_CONV_DIM = SSD_WIDTH + 2 * SSD_GN
IN_PROJ_DIM = FNET_WIDTH + SSD_WIDTH + SSD_CONV_DIM + 2 * SSD_HEADS
D_FF = 11008
FFN_CONV = 3
EPS = 1e-6

kernel_name = 'hybrid_fnet_ssd_convffn_dit_block'


def rmsnorm(x, w):
    xf = x.astype(jnp.float32)
    y = xf * lax.rsqrt(jnp.mean(xf * xf, axis=-1, keepdims=True) + EPS)
    return (y * w.astype(jnp.float32)).astype(x.dtype)


def adaln(cvec, w, b):
    m = jax.nn.silu(cvec) @ w + b
    return jnp.split(m, 6, axis=-1)


def modulate(h, shift, scale):
    return h * (1 + scale[:, None, :]) + shift[:, None, :]


def dwconv1d(x, w, b):
    k = w.shape[0]
    y = lax.conv_general_dilated(x, w[:, None, :], window_strides=(1,),
                                 padding=[((k - 1) // 2, k // 2)],
                                 dimension_numbers=('NWC', 'WIO', 'NWC'),
                                 feature_group_count=x.shape[-1])
    return y + b


def dwconv_grid(x, w, b):
    bsz, n_tok, ch = x.shape
    rows = n_tok // GRID_W
    xg = x.reshape(bsz, rows, GRID_W, ch)
    y = lax.conv_general_dilated(xg, w[:, :, None, :], window_strides=(1, 1), padding='SAME',
                                 dimension_numbers=('NHWC', 'HWIO', 'NHWC'),
                                 feature_group_count=ch)
    return y.reshape(bsz, n_tok, ch) + b


def fourier_mix(u, w_f):
    bsz, n_tok, _ = u.shape
    ug = u.astype(jnp.float32).reshape(bsz, n_tok, FNET_GROUPS, FNET_GROUP_DIM)
    f = jnp.fft.fft2(ug, axes=(1, 3), norm='ortho').real.astype(u.dtype)
    y = jnp.einsum('blgc,gcd->blgd', f, w_f)
    return y.reshape(bsz, n_tok, FNET_WIDTH)


def ssd_inputs(proj, conv_w, conv_b, dt_bias):
    bsz, n_tok, _ = proj.shape
    z = proj[..., :SSD_WIDTH]
    xbc = jax.nn.silu(dwconv1d(proj[..., SSD_WIDTH:SSD_WIDTH + SSD_CONV_DIM], conv_w, conv_b)).astype(jnp.float32)
    xs = xbc[..., :SSD_WIDTH].reshape(bsz, n_tok, SSD_GROUPS, SSD_HPG, SSD_HEAD_DIM)
    bm = xbc[..., SSD_WIDTH:SSD_WIDTH + SSD_GN].reshape(bsz, n_tok, SSD_GROUPS, SSD_STATE)
    cm = xbc[..., SSD_WIDTH + SSD_GN:].reshape(bsz, n_tok, SSD_GROUPS, SSD_STATE)
    dt_raw = proj[..., SSD_WIDTH + SSD_CONV_DIM:].astype(jnp.float32).reshape(bsz, n_tok, 2, SSD_HEADS)
    dt = jax.nn.softplus(dt_raw + dt_bias.astype(jnp.float32)).reshape(bsz, n_tok, 2, SSD_GROUPS, SSD_HPG)
    return z, xs, bm, cm, dt[:, :, 0], dt[:, :, 1]


def ssd_scan(xs, dt, a, bm, cm, h0):
    bsz, n_tok = xs.shape[:2]
    n_chunks = n_tok // SSD_CHUNK

    def chunks(t):
        return jnp.moveaxis(t.reshape(bsz, n_chunks, SSD_CHUNK, *t.shape[2:]), 1, 0)

    log_decay = dt * a
    xdt = xs * dt[..., None]
    lower = jnp.tril(jnp.ones((SSD_CHUNK, SSD_CHUNK), dtype=bool))[None, :, :, None, None]

    def step(h, inp):
        xc, ac, bc, cc = inp
        acum = jnp.cumsum(ac, axis=1)
        seg = acum[:, :, None] - acum[:, None, :]
        decay = jnp.exp(jnp.where(lower, seg, -jnp.inf))
        cb = jnp.einsum('blgn,bsgn->blsg', cc, bc)
        y_diag = jnp.einsum('blsgr,bsgrp->blgrp', cb[..., None] * decay, xc)
        y_off = jnp.einsum('blgn,bgrpn->blgrp', cc, h) * jnp.exp(acum)[..., None]
        to_end = jnp.exp(acum[:, -1:] - acum)
        h_new = h * jnp.exp(acum[:, -1])[..., None, None] + jnp.einsum('bsgn,bsgrp->bgrpn', bc, to_end[..., None] * xc)
        return h_new, y_diag + y_off

    h_final, ys = lax.scan(step, h0, (chunks(xdt), chunks(log_decay), chunks(bm), chunks(cm)))
    return jnp.moveaxis(ys, 0, 1).reshape(xs.shape), h_final


def ssd_direction(xs, dt, a, bm, cm, h0, reverse):
    if reverse:
        xs, dt, bm, cm = jnp.flip(xs, 1), jnp.flip(dt, 1), jnp.flip(bm, 1), jnp.flip(cm, 1)
    y, h = ssd_scan(xs, dt, a, bm, cm, h0)
    if reverse:
        y = jnp.flip(y, 1)
    return y, h


def ssd_output(y, xs, z, d_skip, norm_w):
    bsz, n_tok = y.shape[:2]
    y = y + d_skip.astype(jnp.float32).reshape(SSD_GROUPS, SSD_HPG)[:, :, None] * xs
    g = y.reshape(bsz, n_tok, SSD_WIDTH) * jax.nn.silu(z.astype(jnp.float32))
    g = g.reshape(bsz, n_tok, SSD_GROUPS, SSD_WIDTH // SSD_GROUPS)
    g = g * lax.rsqrt(jnp.mean(g * g, axis=-1, keepdims=True) + EPS)
    return (g.reshape(bsz, n_tok, SSD_WIDTH) * norm_w.astype(jnp.float32)).astype(z.dtype)


def conv_ffn(h, w_gate, w_val, conv_w, conv_b, w_down, on_grid):
    gate = h @ w_gate
    gate = dwconv_grid(gate, conv_w, conv_b) if on_grid else dwconv1d(gate, conv_w[1], conv_b)
    return (jax.nn.silu(gate) * (h @ w_val)) @ w_down


def setup_inputs(seed: int = 0) -> dict:
    key = jax.random.key(seed)
    ks = jax.random.split(key, 24)

    def nrm(k, shape, s):
        return jax.random.normal(k, shape, jnp.float32) * s

    dt0 = jnp.exp(jax.random.uniform(ks[11], (DEPTH, 2, SSD_HEADS), jnp.float32, math.log(1e-3), math.log(1e-1)))
    return {
        'x': nrm(ks[0], (BATCH, SEQ, D_MODEL), 1.0),
        'c': nrm(ks[1], (BATCH, D_MODEL), 1.0),
        'ctx': nrm(ks[2], (BATCH, CTX_LEN, D_MODEL), 1.0),
        'c_ctx': nrm(ks[3], (D_MODEL,), 1.0),
        'ada_w': nrm(ks[4], (DEPTH, D_MODEL, 6 * D_MODEL), 0.5 * D_MODEL ** -0.5),
        'ada_b': nrm(ks[5], (DEPTH, 6 * D_MODEL), 0.02),
        'norm_mix_w': 1.0 + nrm(ks[6], (DEPTH, D_MODEL), 0.02),
        'w_in': nrm(ks[7], (DEPTH, D_MODEL, IN_PROJ_DIM), D_MODEL ** -0.5),
        'fnet_w': nrm(ks[8], (DEPTH, FNET_GROUPS, FNET_GROUP_DIM, FNET_GROUP_DIM), FNET_GROUP_DIM ** -0.5),
        'ssd_conv_w': nrm(ks[9], (DEPTH, SSD_CONV, SSD_CONV_DIM), SSD_CONV ** -0.5),
        'ssd_conv_b': nrm(ks[10], (DEPTH, SSD_CONV_DIM), 0.02),
        'ssd_dt_bias': dt0 + jnp.log(-jnp.expm1(-dt0)),
        'ssd_a_log': jnp.log(jax.random.uniform(ks[12], (DEPTH, 2, SSD_HEADS), jnp.float32, 1.0, 16.0)),
        'ssd_d': 1.0 + nrm(ks[13], (DEPTH, SSD_HEADS), 0.1),
        'ssd_norm_w': 1.0 + nrm(ks[14], (DEPTH, SSD_WIDTH), 0.02),
        'w_out': nrm(ks[15], (DEPTH, D_MIX, D_MODEL), D_MIX ** -0.5),
        'norm_ffn_w': 1.0 + nrm(ks[16], (DEPTH, D_MODEL), 0.02),
        'ffn_w_gate': nrm(ks[17], (DEPTH, D_MODEL, D_FF), D_MODEL ** -0.5),
        'ffn_w_val': nrm(ks[18], (DEPTH, D_MODEL, D_FF), D_MODEL ** -0.5),
        'ffn_conv_w': nrm(ks[19], (DEPTH, FFN_CONV, FFN_CONV, D_FF), 1.0 / FFN_CONV),
        'ffn_conv_b': nrm(ks[20], (DEPTH, D_FF), 0.02),
        'ffn_w_down': nrm(ks[21], (DEPTH, D_FF, D_MODEL), D_FF ** -0.5),
        'norm_final_w': 1.0 + nrm(ks[22], (D_MODEL,), 0.02),
    }


def reference(x, c, ctx, c_ctx, ada_w, ada_b, norm_mix_w, w_in, fnet_w, ssd_conv_w, ssd_conv_b,
              ssd_dt_bias, ssd_a_log, ssd_d, ssd_norm_w, w_out, norm_ffn_w, ffn_w_gate, ffn_w_val,
              ffn_conv_w, ffn_conv_b, ffn_w_down, norm_final_w):
    bsz = x.shape[0]
    for layer in range(DEPTH):
        last = layer == DEPTH - 1
        sh_m, sc_m, g_m, sh_f, sc_f, g_f = adaln(c, ada_w[layer], ada_b[layer])
        csh_m, csc_m, cg_m, csh_f, csc_f, cg_f = adaln(c_ctx[None], ada_w[layer], ada_b[layer])
        a_neg = -jnp.exp(ssd_a_log[layer].astype(jnp.float32)).reshape(2, SSD_GROUPS, SSD_HPG)
        conv_p = (ssd_conv_w[layer], ssd_conv_b[layer], ssd_dt_bias[layer])

        hc = modulate(rmsnorm(ctx, norm_mix_w[layer]), csh_m, csc_m)
        zc, xc, bc, cc, dtc_f, dtc_b = ssd_inputs(hc @ w_in[layer][:, FNET_WIDTH:], *conv_p)
        h0 = jnp.zeros((bsz, SSD_GROUPS, SSD_HPG, SSD_HEAD_DIM, SSD_STATE), jnp.float32)
        yc_f, hc_f = ssd_direction(xc, dtc_f, a_neg[0], bc, cc, h0, False)
        yc_b, hc_b = ssd_direction(xc, dtc_b, a_neg[1], bc, cc, h0, True)

        hx = modulate(rmsnorm(x, norm_mix_w[layer]), sh_m, sc_m)
        px = hx @ w_in[layer]
        f_out = fourier_mix(px[..., :FNET_WIDTH], fnet_w[layer])
        zx, xx, bx, cx, dtx_f, dtx_b = ssd_inputs(px[..., FNET_WIDTH:], *conv_p)
        yx_f, _ = ssd_direction(xx, dtx_f, a_neg[0], bx, cx, hc_f, False)
        yx_b, _ = ssd_direction(xx, dtx_b, a_neg[1], bx, cx, hc_b, True)
        s_out = ssd_output(yx_f + yx_b, xx, zx, ssd_d[layer], ssd_norm_w[layer])
        x = x + g_m[:, None, :] * (jnp.concatenate([f_out, s_out], axis=-1) @ w_out[layer])

        hx = modulate(rmsnorm(x, norm_ffn_w[layer]), sh_f, sc_f)
        x = x + g_f[:, None, :] * conv_ffn(hx, ffn_w_gate[layer], ffn_w_val[layer], ffn_conv_w[layer],
                                           ffn_conv_b[layer], ffn_w_down[layer], True)

        if not last:
            fc_out = fourier_mix(hc @ w_in[layer][:, :FNET_WIDTH], fnet_w[layer])
            sc_out = ssd_output(yc_f + yc_b, xc, zc, ssd_d[layer], ssd_norm_w[layer])
            ctx = ctx + cg_m[:, None, :] * (jnp.concatenate([fc_out, sc_out], axis=-1) @ w_out[layer])
            hc2 = modulate(rmsnorm(ctx, norm_ffn_w[layer]), csh_f, csc_f)
            ctx = ctx + cg_f[:, None, :] * conv_ffn(hc2, ffn_w_gate[layer], ffn_w_val[layer], ffn_conv_w[layer],
                                                    ffn_conv_b[layer], ffn_w_down[layer], False)
    return rmsnorm(x, norm_final_w)
```

```python
import functools
import math

import numpy as np
import jax
import jax.numpy as jnp
from jax import lax
from jax.experimental import pallas as pl
from jax.experimental.pallas import tpu as pltpu

F32 = jnp.float32
BF16 = jnp.bfloat16
HIGHEST = lax.Precision.HIGHEST

GRID_W = 64
SSD_HEAD_DIM = 64
SSD_GROUPS = 8
SSD_STATE = 128
SSD_CHUNK = 128
EPS = 1e-6

LANES = 128
SUBLANES = 8
VMEM_LIMIT = 56 * 1024 * 1024


def _cparams(*sem):
    return pltpu.CompilerParams(dimension_semantics=sem, vmem_limit_bytes=VMEM_LIMIT)


def _silu(v):
    return v * jax.nn.sigmoid(v)


def _adaln_kernel(c_ref, w_ref, b_ref, o_ref):
    c = c_ref[...]
    s = _silu(c).astype(BF16)
    o_ref[...] = jnp.dot(s, w_ref[...].astype(BF16), preferred_element_type=F32) + b_ref[...]


def _adaln(cvecs, w, b):
    d, n = w.shape
    tn = 512
    return pl.pallas_call(
        _adaln_kernel,
        grid=(n // tn,),
        in_specs=[pl.BlockSpec((SUBLANES, d), lambda j: (0, 0)),
                  pl.BlockSpec((d, tn), lambda j: (0, j)),
                  pl.BlockSpec((1, tn), lambda j: (0, j))],
        out_specs=pl.BlockSpec((SUBLANES, tn), lambda j: (0, j)),
        out_shape=jax.ShapeDtypeStruct((SUBLANES, n), F32),
        compiler_params=_cparams("parallel"),
        name="adaln",
    )(cvecs, w, b.reshape(1, n))


def _lnmod_kernel(x_ref, w_ref, sh_ref, sc_ref, o_ref):
    x = x_ref[...]
    r = lax.rsqrt(jnp.mean(x * x, axis=-1, keepdims=True) + EPS)
    o_ref[...] = ((x * r) * w_ref[...] * (1.0 + sc_ref[0]) + sh_ref[0]).astype(o_ref.dtype)


def _lnmod(x2d, w, shift, scale, seq, tm):
    t, d = x2d.shape
    bsz = shift.shape[0]
    per = seq // tm
    return pl.pallas_call(
        _lnmod_kernel,
        grid=(t // tm,),
        in_specs=[pl.BlockSpec((tm, d), lambda i: (i, 0)),
                  pl.BlockSpec((1, d), lambda i: (0, 0)),
                  pl.BlockSpec((1, 1, d), lambda i: (i // per, 0, 0)),
                  pl.BlockSpec((1, 1, d), lambda i: (i // per, 0, 0))],
        out_specs=pl.BlockSpec((tm, d), lambda i: (i, 0)),
        out_shape=jax.ShapeDtypeStruct((t, d), BF16),
        compiler_params=_cparams("parallel"),
        name="lnmod",
    )(x2d, w.reshape(1, d), shift.reshape(bsz, 1, d), scale.reshape(bsz, 1, d))


def _rmsnorm_kernel(x_ref, w_ref, o_ref):
    x = x_ref[...]
    r = lax.rsqrt(jnp.mean(x * x, axis=-1, keepdims=True) + EPS)
    o_ref[...] = (x * r) * w_ref[...]


def _rmsnorm(x2d, w, tm):
    t, d = x2d.shape
    return pl.pallas_call(
        _rmsnorm_kernel,
        grid=(t // tm,),
        in_specs=[pl.BlockSpec((tm, d), lambda i: (i, 0)),
                  pl.BlockSpec((1, d), lambda i: (0, 0))],
        out_specs=pl.BlockSpec((tm, d), lambda i: (i, 0)),
        out_shape=jax.ShapeDtypeStruct((t, d), F32),
        compiler_params=_cparams("parallel"),
        name="final_rmsnorm",
    )(x2d, w.reshape(1, d))


def _mm_kernel(a_ref, w_ref, o_ref):
    o_ref[...] = jnp.dot(a_ref[...], w_ref[...], preferred_element_type=F32).astype(o_ref.dtype)


def _mm(a, w, tm, tn, out_dtype, name):
    m, k = a.shape
    n = w.shape[1]
    return pl.pallas_call(
        _mm_kernel,
        grid=(m // tm, n // tn),
        in_specs=[pl.BlockSpec((tm, k), lambda i, j: (i, 0)),
                  pl.BlockSpec((k, tn), lambda i, j: (0, j))],
        out_specs=pl.BlockSpec((tm, tn), lambda i, j: (i, j)),
        out_shape=jax.ShapeDtypeStruct((m, n), out_dtype),
        compiler_params=_cparams("parallel", "arbitrary"),
        name=name,
    )(a, w)


def _mm_pair_kernel(a_ref, w1_ref, w2_ref, o1_ref, o2_ref):
    a = a_ref[...]
    o1_ref[...] = jnp.dot(a, w1_ref[...], preferred_element_type=F32).astype(o1_ref.dtype)
    o2_ref[...] = jnp.dot(a, w2_ref[...], preferred_element_type=F32).astype(o2_ref.dtype)


def _mm_pair(a, w1, w2, tm, tn, dt1, dt2, name):
    m, k = a.shape
    n = w1.shape[1]
    return pl.pallas_call(
        _mm_pair_kernel,
        grid=(m // tm, n // tn),
        in_specs=[pl.BlockSpec((tm, k), lambda i, j: (i, 0)),
                  pl.BlockSpec((k, tn), lambda i, j: (0, j)),
                  pl.BlockSpec((k, tn), lambda i, j: (0, j))],
        out_specs=[pl.BlockSpec((tm, tn), lambda i, j: (i, j)),
                   pl.BlockSpec((tm, tn), lambda i, j: (i, j))],
        out_shape=[jax.ShapeDtypeStruct((m, n), dt1), jax.ShapeDtypeStruct((m, n), dt2)],
        compiler_params=_cparams("parallel", "arbitrary"),
        name=name,
    )(a, w1, w2)


def _dft_tables(n):
    idx = np.arange(n)
    ang = 2.0 * np.pi * ((idx[:, None] * idx[None, :]) % n) / n
    return np.cos(ang), np.sin(ang)


def _fold_kernel(cs_ref, wf_ref, o_ref):
    o_ref[0] = jnp.dot(cs_ref[...], wf_ref[0], precision=HIGHEST, preferred_element_type=F32)


def _fourier_fold(fnet_w, seq):
    g, c, _ = fnet_w.shape
    cc, sc = _dft_tables(c)
    cs = jnp.asarray(np.concatenate([cc, sc], axis=0) / math.sqrt(seq * c), F32)
    return pl.pallas_call(
        _fold_kernel,
        grid=(g,),
        in_specs=[pl.BlockSpec((2 * c, c), lambda i: (0, 0)),
                  pl.BlockSpec((1, c, c), lambda i: (i, 0, 0))],
        out_specs=pl.BlockSpec((1, 2 * c, c), lambda i: (i, 0, 0)),
        out_shape=jax.ShapeDtypeStruct((g, 2 * c, c), F32),
        compiler_params=_cparams("parallel"),
        name="fnet_fold",
    )(cs, fnet_w)


def _dft1_kernel(f_ref, u_ref, zr_ref, zi_ref):
    n1 = zr_ref.shape[1]
    z = jnp.dot(f_ref[...], u_ref[0], precision=HIGHEST, preferred_element_type=F32)
    zr_ref[0] = z[:n1]
    zi_ref[0] = z[n1:]


def _dft_stage1(u3, n1):
    bsz, _, cols = u3.shape
    c1, s1 = _dft_tables(n1)
    f1 = jnp.asarray(np.concatenate([c1, -s1], axis=0), F32)
    tc = 8192
    spec = pl.BlockSpec((1, n1, tc), lambda b, j: (b, 0, j))
    return pl.pallas_call(
        _dft1_kernel,
        grid=(bsz, cols // tc),
        in_specs=[pl.BlockSpec((2 * n1, n1), lambda b, j: (0, 0)), spec],
        out_specs=[spec, spec],
        out_shape=[jax.ShapeDtypeStruct(u3.shape, F32)] * 2,
        compiler_params=_cparams("parallel", "parallel"),
        name="fnet_dft_stage1",
    )(f1, u3)


def _dft2_kernel(c2_ref, s2_ref, tc_ref, ts_ref, wcs_ref, zr_ref, zi_ref, o_ref, *, kb, groups, cg):
    c2 = c2_ref[...]
    s2 = s2_ref[...]
    for j in range(kb):
        tcj = tc_ref[j:j + 1, :]
        tsj = ts_ref[j:j + 1, :]
        gc = c2 * tcj - s2 * tsj
        gs = s2 * tcj + c2 * tsj
        zr = zr_ref[0, j]
        zi = zi_ref[0, j]
        dot = functools.partial(jnp.dot, precision=HIGHEST, preferred_element_type=F32)
        pr = dot(gc, zr) + dot(gs, zi)
        pi = dot(gc, zi) - dot(gs, zr)
        for g in range(groups):
            sl = slice(g * cg, (g + 1) * cg)
            w = wcs_ref[g]
            y = dot(pr[:, sl], w[:cg]) + dot(pi[:, sl], w[cg:])
            o_ref[0, :, j, sl] = y.astype(o_ref.dtype)


def _dft_stage2(zr4, zi4, wcs, seq):
    bsz, n1, n2, wdt = zr4.shape
    groups, _, cg = wcs.shape
    c2, s2 = _dft_tables(n2)
    k1 = np.arange(n1)[:, None]
    l2 = np.arange(n2)[None, :]
    ang = 2.0 * np.pi * ((k1 * l2) % seq) / seq
    kb = SUBLANES
    zspec = pl.BlockSpec((1, kb, n2, wdt), lambda b, i: (b, i, 0, 0))
    full = lambda shape: pl.BlockSpec(shape, lambda b, i: (0,) * len(shape))
    return pl.pallas_call(
        functools.partial(_dft2_kernel, kb=kb, groups=groups, cg=cg),
        grid=(bsz, n1 // kb),
        in_specs=[full((n2, n2)), full((n2, n2)),
                  pl.BlockSpec((kb, n2), lambda b, i: (i, 0)),
                  pl.BlockSpec((kb, n2), lambda b, i: (i, 0)),
                  full(wcs.shape), zspec, zspec],
        out_specs=pl.BlockSpec((1, n2, kb, wdt), lambda b, i: (b, 0, i, 0)),
        out_shape=jax.ShapeDtypeStruct((bsz, n2, n1, wdt), BF16),
        compiler_params=_cparams("parallel", "parallel"),
        name="fnet_dft_stage2",
    )(jnp.asarray(c2, F32), jnp.asarray(s2, F32), jnp.asarray(np.cos(ang), F32),
      jnp.asarray(np.sin(ang), F32), wcs, zr4, zi4)


def _fourier_mix(u, fnet_w, bsz, seq):
    wdt = u.shape[1]
    n2 = LANES
    n1 = seq // n2
    wcs = _fourier_fold(fnet_w, seq)
    zr, zi = _dft_stage1(u.reshape(bsz, n1, n2 * wdt), n1)
    out = _dft_stage2(zr.reshape(bsz, n1, n2, wdt), zi.reshape(bsz, n1, n2, wdt), wcs, seq)
    return out.reshape(bsz * seq, wdt)


def _dwconv_silu_kernel(prev_ref, main_ref, next_ref, w_ref, b_ref, o_ref, scr, *, per, tl, taps):
    t = pl.program_id(0) % per
    halo = SUBLANES
    scr[0:halo, :] = jnp.where(t == 0, 0.0, prev_ref[...])
    scr[halo:halo + tl, :] = main_ref[...]
    scr[halo + tl:2 * halo + tl, :] = jnp.where(t == per - 1, 0.0, next_ref[...])
    acc = b_ref[...] + w_ref[0:1, :] * scr[pl.ds(halo - (taps - 1) // 2, tl), :]
    for k in range(1, taps):
        acc = acc + w_ref[k:k + 1, :] * scr[pl.ds(halo - (taps - 1) // 2 + k, tl), :]
    o_ref[...] = _silu(acc).astype(o_ref.dtype)


def _dwconv_silu(src, col0, width, conv_w, conv_b, seq, tl, tc):
    t = src.shape[0]
    taps = conv_w.shape[0]
    per = seq // tl
    cb0 = col0 // tc
    hb = tl // SUBLANES
    last = t // SUBLANES - 1
    return pl.pallas_call(
        functools.partial(_dwconv_silu_kernel, per=per, tl=tl, taps=taps),
        grid=(t // tl, width // tc),
        in_specs=[pl.BlockSpec((SUBLANES, tc), lambda i, j: (jnp.maximum(i * hb - 1, 0), cb0 + j)),
                  pl.BlockSpec((tl, tc), lambda i, j: (i, cb0 + j)),
                  pl.BlockSpec((SUBLANES, tc), lambda i, j: (jnp.minimum((i + 1) * hb, last), cb0 + j)),
                  pl.BlockSpec((taps, tc), lambda i, j: (0, j)),
                  pl.BlockSpec((1, tc), lambda i, j: (0, j))],
        out_specs=pl.BlockSpec((tl, tc), lambda i, j: (i, j)),
        out_shape=jax.ShapeDtypeStruct((t, width), F32),
        scratch_shapes=[pltpu.VMEM((tl + 2 * SUBLANES, tc), F32)],
        compiler_params=_cparams("parallel", "parallel"),
        name="ssd_dwconv_silu",
    )(src, src, src, conv_w, conv_b.reshape(1, width))


def _ssd_prep_kernel(raw_ref, bias_ref, alog_ref, dtT_ref, acol_ref, aT_ref, w_ref, *, heads):
    c = SSD_CHUNK
    v = raw_ref[...] + bias_ref[...]
    dt = jnp.maximum(v, 0.0) + jnp.log1p(jnp.exp(-jnp.abs(v)))
    lane = lax.broadcasted_iota(jnp.int32, (c, LANES), 1)
    dt = jnp.where(lane < 2 * heads, dt, 0.0)
    dta = dt * (-jnp.exp(alog_ref[...]))
    li = lax.broadcasted_iota(jnp.int32, (c, c), 0)
    si = lax.broadcasted_iota(jnp.int32, (c, c), 1)
    lower = (si <= li).astype(F32)
    upper = (si >= li).astype(F32)
    pre = jnp.dot(lower, dta, precision=HIGHEST, preferred_element_type=F32)
    suf = jnp.dot(upper, dta, precision=HIGHEST, preferred_element_type=F32)
    acum = jnp.where(lane < heads, pre, suf)
    lane_row = lax.broadcasted_iota(jnp.int32, (1, LANES), 1)
    aend = jnp.where(lane_row < heads, acum[c - 1:c, :], acum[0:1, :])
    dtT_ref[...] = dt.T
    acol_ref[...] = acum
    aT_ref[...] = acum.T
    w_ref[...] = dt * jnp.exp(aend - acum)


def _ssd_prep(src, colblk, dt_bias, a_log, heads):
    t = src.shape[0]
    c = SSD_CHUNK
    pad = LANES - 2 * heads
    bias = jnp.pad(dt_bias.reshape(1, 2 * heads), ((0, 0), (0, pad)))
    alog = jnp.pad(a_log.reshape(1, 2 * heads), ((0, 0), (0, pad)))
    spec = pl.BlockSpec((c, LANES), lambda i: (i, 0))
    vec = pl.BlockSpec((1, LANES), lambda i: (0, 0))
    return pl.pallas_call(
        functools.partial(_ssd_prep_kernel, heads=heads),
        grid=(t // c,),
        in_specs=[pl.BlockSpec((c, LANES), lambda i: (i, colblk)), vec, vec],
        out_specs=[spec] * 4,
        out_shape=[jax.ShapeDtypeStruct((t, LANES), F32)] * 4,
        compiler_params=_cparams("parallel"),
        name="ssd_prep",
    )(src, bias, alog)


def _ssd_scan_kernel(x_ref, b_ref, c_ref, dtT_ref, acol_ref, aT_ref, w_ref, h0_ref, y_ref, hout_ref, h_scr,
                     *, reverse, col0, groups, hpg, nsteps):
    c = SSD_CHUNK
    n = SSD_STATE
    s = pl.program_id(1)

    @pl.when(s == 0)
    def _():
        h_scr[...] = h0_ref[0]

    li = lax.broadcasted_iota(jnp.int32, (c, c), 0)
    si = lax.broadcasted_iota(jnp.int32, (c, c), 1)
    mask = (li <= si) if reverse else (li >= si)
    lo_half = lax.broadcasted_iota(jnp.int32, (c, LANES), 1) < SSD_HEAD_DIM
    lo_half_row = lax.broadcasted_iota(jnp.int32, (1, LANES), 1) < SSD_HEAD_DIM
    acol = acol_ref[...]
    a_t = aT_ref[...]
    dt_t = dtT_ref[...]
    wcol = w_ref[...]
    ecol = jnp.exp(acol)
    eend = ecol[0:1, :] if reverse else ecol[c - 1:c, :]
    pairs = hpg // 2
    for g in range(groups):
        bg = b_ref[:, g * n:(g + 1) * n].astype(BF16)
        cg = c_ref[:, g * n:(g + 1) * n].astype(BF16)
        cb = lax.dot_general(cg, bg, (((1,), (1,)), ((), ())), preferred_element_type=F32)
        h_t = h_scr[g]
        yoff = jnp.dot(cg, h_t.astype(BF16), preferred_element_type=F32)
        for pr in range(pairs):
            lanes = slice((g * pairs + pr) * LANES, (g * pairs + pr + 1) * LANES)
            loc = slice(pr * LANES, (pr + 1) * LANES)
            xp = x_ref[:, lanes]
            xpb = xp.astype(BF16)
            k0 = col0 + g * hpg + 2 * pr
            ys = []
            for q in range(2):
                k = k0 + q
                seg = acol[:, k:k + 1] - a_t[k:k + 1, :]
                dec = jnp.exp(jnp.where(mask, seg, -jnp.inf))
                m = (cb * dec * dt_t[k:k + 1, :]).astype(BF16)
                ys.append(jnp.dot(m, xpb, preferred_element_type=F32))
            e_pair = jnp.where(lo_half, ecol[:, k0:k0 + 1], ecol[:, k0 + 1:k0 + 2])
            y_ref[:, lanes] = jnp.where(lo_half, ys[0], ys[1]) + yoff[:, loc] * e_pair
            w_pair = jnp.where(lo_half, wcol[:, k0:k0 + 1], wcol[:, k0 + 1:k0 + 2])
            xw = (xp * w_pair).astype(BF16)
            upd = lax.dot_general(bg, xw, (((0,), (0,)), ((), ())), preferred_element_type=F32)
            eend_pair = jnp.where(lo_half_row, eend[:, k0:k0 + 1], eend[:, k0 + 1:k0 + 2])
            h_scr[g, :, loc] = h_t[:, loc] * eend_pair + upd

    @pl.when(s == nsteps - 1)
    def _():
        hout_ref[0] = h_scr[...]


def _ssd_scan(act, prep, h0, seq, heads, reverse):
    t = act.shape[0]
    bsz = t // seq
    c, n, groups = SSD_CHUNK, SSD_STATE, SSD_GROUPS
    hpg = heads // groups
    width = heads * SSD_HEAD_DIM
    gn = groups * n
    nsteps = seq // c
    dt_t, acol, a_t, wcol = prep

    def row(b, s):
        return b * nsteps + ((nsteps - 1 - s) if reverse else s)

    small = pl.BlockSpec((c, LANES), lambda b, s: (row(b, s), 0))
    hspec = pl.BlockSpec((1, groups, n, hpg * SSD_HEAD_DIM), lambda b, s: (b, 0, 0, 0))
    return pl.pallas_call(
        functools.partial(_ssd_scan_kernel, reverse=reverse, col0=heads if reverse else 0,
                          groups=groups, hpg=hpg, nsteps=nsteps),
        grid=(bsz, nsteps),
        in_specs=[pl.BlockSpec((c, width), lambda b, s: (row(b, s), 0)),
                  pl.BlockSpec((c, gn), lambda b, s: (row(b, s), width // gn)),
                  pl.BlockSpec((c, gn), lambda b, s: (row(b, s), width // gn + 1)),
                  small, small, small, small, hspec],
        out_specs=[pl.BlockSpec((c, width), lambda b, s: (row(b, s), 0)), hspec],
        out_shape=[jax.ShapeDtypeStruct((t, width), F32),
                   jax.ShapeDtypeStruct(h0.shape, F32)],
        scratch_shapes=[pltpu.VMEM((groups, n, hpg * SSD_HEAD_DIM), F32)],
        compiler_params=_cparams("parallel", "arbitrary"),
        name="ssd_scan_bwd" if reverse else "ssd_scan_fwd",
    )(act, act, act, dt_t, acol, a_t, wcol, h0)


def _ssd_out_kernel(yf_ref, yb_ref, x_ref, z_ref, d_ref, nw_ref, o_ref, *, groups, gw):
    y = yf_ref[...] + yb_ref[...] + d_ref[...] * x_ref[...]
    g = y * _silu(z_ref[...])
    for i in range(groups):
        sl = slice(i * gw, (i + 1) * gw)
        gi = g[:, sl]
        r = lax.rsqrt(jnp.mean(gi * gi, axis=-1, keepdims=True) + EPS)
        o_ref[:, sl] = (gi * r * nw_ref[:, sl]).astype(o_ref.dtype)


def _ssd_out(yf, yb, act, rest, d_exp, norm_w, tm):
    t, width = yf.shape
    gw = width // SSD_GROUPS
    row = pl.BlockSpec((tm, width), lambda i: (i, 0))
    vec = pl.BlockSpec((1, width), lambda i: (0, 0))
    return pl.pallas_call(
        functools.partial(_ssd_out_kernel, groups=SSD_GROUPS, gw=gw),
        grid=(t // tm,),
        in_specs=[row, row, row, row, vec, vec],
        out_specs=row,
        out_shape=jax.ShapeDtypeStruct((t, width), BF16),
        compiler_params=_cparams("parallel"),
        name="ssd_out",
    )(yf, yb, act, rest, d_exp.reshape(1, width), norm_w.reshape(1, width))


def _wout_kernel(f_ref, s_ref, w1_ref, w2_ref, x_ref, g_ref, o_ref):
    acc = jnp.dot(f_ref[...], w1_ref[...], preferred_element_type=F32)
    acc = acc + jnp.dot(s_ref[...], w2_ref[...], preferred_element_type=F32)
    o_ref[...] = x_ref[...] + g_ref[0] * acc


def _wout(f_out, s_out, w1, w2, x2d, gate, seq, tm, tn):
    t, d = x2d.shape
    bsz = gate.shape[0]
    k1, k2 = f_out.shape[1], s_out.shape[1]
    per = seq // tm
    return pl.pallas_call(
        _wout_kernel,
        grid=(t // tm, d // tn),
        in_specs=[pl.BlockSpec((tm, k1), lambda i, j: (i, 0)),
                  pl.BlockSpec((tm, k2), lambda i, j: (i, 0)),
                  pl.BlockSpec((k1, tn), lambda i, j: (0, j)),
                  pl.BlockSpec((k2, tn), lambda i, j: (0, j)),
                  pl.BlockSpec((tm, tn), lambda i, j: (i, j)),
                  pl.BlockSpec((1, 1, tn), lambda i, j: (i // per, 0, j))],
        out_specs=pl.BlockSpec((tm, tn), lambda i, j: (i, j)),
        out_shape=jax.ShapeDtypeStruct((t, d), F32),
        compiler_params=_cparams("parallel", "arbitrary"),
        name="out_proj",
    )(f_out, s_out, w1, w2, x2d, gate.reshape(bsz, 1, d))


def _ffn_act_kernel(up_ref, main_ref, dn_ref, val_ref, w_ref, b_ref, o_ref, scr, *, per, tl):
    t = pl.program_id(0) % per
    gw = GRID_W
    m = SUBLANES
    tc = scr.shape[1]
    zeros = jnp.zeros((m, tc), F32)
    scr[0:m, :] = zeros
    scr[m:m + gw, :] = jnp.where(t == 0, 0.0, up_ref[...])
    scr[m + gw:m + gw + tl, :] = main_ref[...]
    scr[m + gw + tl:m + 2 * gw + tl, :] = jnp.where(t == per - 1, 0.0, dn_ref[...])
    scr[m + 2 * gw + tl:2 * m + 2 * gw + tl, :] = zeros
    col = lax.broadcasted_iota(jnp.int32, (tl, tc), 0) & (gw - 1)
    not_first = col != 0
    not_last = col != gw - 1
    acc = jnp.broadcast_to(b_ref[...], (tl, tc))
    for dy in range(3):
        for dx in range(3):
            tap = scr[pl.ds(m + gw * dy + dx - 1, tl), :]
            if dx == 0:
                tap = jnp.where(not_first, tap, 0.0)
            elif dx == 2:
                tap = jnp.where(not_last, tap, 0.0)
            acc = acc + w_ref[dy * 3 + dx:dy * 3 + dx + 1, :] * tap
    o_ref[...] = (_silu(acc) * val_ref[...].astype(F32)).astype(o_ref.dtype)


def _ffn_act(gate, val, conv_w9, conv_b, seq, tl, tc):
    t, f = gate.shape
    per = seq // tl
    hb = tl // GRID_W
    last = t // GRID_W - 1
    return pl.pallas_call(
        functools.partial(_ffn_act_kernel, per=per, tl=tl),
        grid=(t // tl, f // tc),
        in_specs=[pl.BlockSpec((GRID_W, tc), lambda i, j: (jnp.maximum(i * hb - 1, 0), j)),
                  pl.BlockSpec((tl, tc), lambda i, j: (i, j)),
                  pl.BlockSpec((GRID_W, tc), lambda i, j: (jnp.minimum((i + 1) * hb, last), j)),
                  pl.BlockSpec((tl, tc), lambda i, j: (i, j)),
                  pl.BlockSpec((9, tc), lambda i, j: (0, j)),
                  pl.BlockSpec((1, tc), lambda i, j: (0, j))],
        out_specs=pl.BlockSpec((tl, tc), lambda i, j: (i, j)),
        out_shape=jax.ShapeDtypeStruct((t, f), BF16),
        scratch_shapes=[pltpu.VMEM((tl + 2 * GRID_W + 2 * SUBLANES, tc), F32)],
        compiler_params=_cparams("parallel", "parallel"),
        name="ffn_conv_act",
    )(gate, gate, gate, val, conv_w9, conv_b.reshape(1, f))


def _down_kernel(a_ref, w_ref, x_ref, g_ref, o_ref, acc_ref, *, nk):
    k = pl.program_id(2)

    @pl.when(k == 0)
    def _():
        acc_ref[...] = jnp.zeros_like(acc_ref)

    acc_ref[...] += jnp.dot(a_ref[...], w_ref[...], preferred_element_type=F32)

    @pl.when(k == nk - 1)
    def _():
        o_ref[...] = x_ref[...] + g_ref[0] * acc_ref[...]


def _down(a, w, x2d, gate, seq, tm, tn, tk):
    t, d = x2d.shape
    f = a.shape[1]
    bsz = gate.shape[0]
    per = seq // tm
    nk = f // tk
    return pl.pallas_call(
        functools.partial(_down_kernel, nk=nk),
        grid=(t // tm, d // tn, nk),
        in_specs=[pl.BlockSpec((tm, tk), lambda i, j, k: (i, k)),
                  pl.BlockSpec((tk, tn), lambda i, j, k: (k, j)),
                  pl.BlockSpec((tm, tn), lambda i, j, k: (i, j)),
                  pl.BlockSpec((1, 1, tn), lambda i, j, k: (i // per, 0, j))],
        out_specs=pl.BlockSpec((tm, tn), lambda i, j, k: (i, j)),
        out_shape=jax.ShapeDtypeStruct((t, d), F32),
        scratch_shapes=[pltpu.VMEM((tm, tn), F32)],
        compiler_params=_cparams("parallel", "parallel", "arbitrary"),
        name="ffn_down",
    )(a, w, x2d, gate.reshape(bsz, 1, d))


def _tile(n, pref):
    t = pref
    while n % t:
        t //= 2
    return t


def _ssd_branch_inputs(h, w_rest, conv_w, conv_b, dt_bias, a_log, seq, fw, heads, tm):
    width = heads * SSD_HEAD_DIM
    conv_dim = conv_w.shape[1]
    rest = _mm(h, w_rest, tm, 640 if w_rest.shape[1] % 640 == 0 else LANES, F32, "in_proj_ssd")
    act = _dwconv_silu(rest, width, conv_dim, conv_w, conv_b, seq, _tile(seq, 512), 512)
    prep = _ssd_prep(rest, (width + conv_dim) // LANES, dt_bias, a_log, heads)
    return rest, act, prep


def kernel(x, c, ctx, c_ctx, ada_w, ada_b, norm_mix_w, w_in, fnet_w, ssd_conv_w, ssd_conv_b, ssd_dt_bias,
           ssd_a_log, ssd_d, ssd_norm_w, w_out, norm_ffn_w, ffn_w_gate, ffn_w_val, ffn_conv_w, ffn_conv_b,
           ffn_w_down, norm_final_w):
    bsz, seq, d = x.shape
    ctx_len = ctx.shape[1]
    depth = ada_w.shape[0]
    fw = fnet_w.shape[1] * fnet_w.shape[2]
    width = d - fw
    heads = width // SSD_HEAD_DIM
    conv_dim = ssd_conv_w.shape[2]
    assert depth == 1, "only the single-layer block is implemented"
    assert 2 * heads <= LANES and heads % SSD_GROUPS == 0 and (heads // SSD_GROUPS) % 2 == 0
    t = bsz * seq
    layer = 0

    cvecs = jnp.zeros((SUBLANES, d), F32).at[:bsz].set(c).at[bsz].set(c_ctx)
    mod = _adaln(cvecs, ada_w[layer], ada_b[layer])
    sh_m, sc_m, g_m, sh_f, sc_f, g_f = [mod[:bsz, i * d:(i + 1) * d] for i in range(6)]
    csh_m, csc_m = [jnp.broadcast_to(mod[bsz:bsz + 1, i * d:(i + 1) * d], (bsz, d)) for i in range(2)]

    w_in_l = w_in[layer]
    w_u = w_in_l[:, :fw].astype(BF16)
    n_rest = w_in_l.shape[1] - fw
    w_rest = jnp.pad(w_in_l[:, fw:], ((0, 0), (0, -n_rest % LANES))).astype(BF16)
    conv_p = (ssd_conv_w[layer], ssd_conv_b[layer], ssd_dt_bias[layer], ssd_a_log[layer])

    x2d = x.reshape(t, d)
    hc = _lnmod(ctx.reshape(bsz * ctx_len, d), norm_mix_w[layer], csh_m, csc_m, ctx_len, ctx_len)
    _, act_c, prep_c = _ssd_branch_inputs(hc, w_rest, *conv_p, ctx_len, fw, heads, _tile(bsz * ctx_len, 1024))
    h_zero = jnp.zeros((bsz, SSD_GROUPS, SSD_STATE, (heads // SSD_GROUPS) * SSD_HEAD_DIM), F32)
    _, hc_f = _ssd_scan(act_c, prep_c, h_zero, ctx_len, heads, False)
    _, hc_b = _ssd_scan(act_c, prep_c, h_zero, ctx_len, heads, True)

    tm = _tile(seq, 1024)
    tr = _tile(seq, 256)
    hx = _lnmod(x2d, norm_mix_w[layer], sh_m, sc_m, seq, tr)
    u = _mm(hx, w_u, tm, 512, F32, "in_proj_fnet")
    f_out = _fourier_mix(u, fnet_w[layer], bsz, seq)
    rest, act, prep = _ssd_branch_inputs(hx, w_rest, *conv_p, seq, fw, heads, tm)
    y_f, _ = _ssd_scan(act, prep, hc_f, seq, heads, False)
    y_b, _ = _ssd_scan(act, prep, hc_b, seq, heads, True)
    d_exp = jnp.repeat(ssd_d[layer], SSD_HEAD_DIM)
    s_out = _ssd_out(y_f, y_b, act, rest, d_exp, ssd_norm_w[layer], tr)
    w_out_l = w_out[layer].astype(BF16)
    x1 = _wout(f_out, s_out, w_out_l[:fw], w_out_l[fw:], x2d, g_m, seq, tm, 512)

    hx2 = _lnmod(x1, norm_ffn_w[layer], sh_f, sc_f, seq, tr)
    f = ffn_w_gate.shape[2]
    gate, val = _mm_pair(hx2, ffn_w_gate[layer].astype(BF16), ffn_w_val[layer].astype(BF16),
                         tm, 256, F32, BF16, "ffn_up")
    a = _ffn_act(gate, val, ffn_conv_w[layer].reshape(9, f), ffn_conv_b[layer], seq, _tile(seq, 512), 256)
    x2 = _down(a, ffn_w_down[layer].astype(BF16), x1, g_f, seq, _tile(seq, 512), 2048, 256)
    return _rmsnorm(x2, norm_final_w, tr).reshape(bsz, seq, d)
```

```python
import functools
import math

import numpy as np
import jax
import jax.numpy as jnp
from jax import lax
from jax.experimental import pallas as pl
from jax.experimental.pallas import tpu as pltpu

F32 = jnp.float32
BF16 = jnp.bfloat16
HIGHEST = lax.Precision.HIGHEST

GRID_W = 64
SSD_HEAD_DIM = 64
SSD_GROUPS = 8
SSD_STATE = 128
SSD_CHUNK = 128
EPS = 1e-6

LANES = 128
SUBLANES = 8
VMEM_LIMIT = 56 * 1024 * 1024
FFN_TILE = 512
FFN_DOWN_SLAB = 512


def _cparams(*sem):
    return pltpu.CompilerParams(dimension_semantics=sem, vmem_limit_bytes=VMEM_LIMIT)


def _silu(v):
    return v * jax.nn.sigmoid(v)


def _adaln_kernel(c_ref, w_ref, b_ref, o_ref):
    c = c_ref[...]
    s = _silu(c).astype(BF16)
    o_ref[...] = jnp.dot(s, w_ref[...].astype(BF16), preferred_element_type=F32) + b_ref[...]


def _adaln(cvecs, w, b):
    d, n = w.shape
    tn = 512
    return pl.pallas_call(
        _adaln_kernel,
        grid=(n // tn,),
        in_specs=[pl.BlockSpec((SUBLANES, d), lambda j: (0, 0)),
                  pl.BlockSpec((d, tn), lambda j: (0, j)),
                  pl.BlockSpec((1, tn), lambda j: (0, j))],
        out_specs=pl.BlockSpec((SUBLANES, tn), lambda j: (0, j)),
        out_shape=jax.ShapeDtypeStruct((SUBLANES, n), F32),
        compiler_params=_cparams("parallel"),
        name="adaln",
    )(cvecs, w, b.reshape(1, n))


def _lnmod_kernel(x_ref, w_ref, sh_ref, sc_ref, o_ref):
    x = x_ref[...]
    r = lax.rsqrt(jnp.mean(x * x, axis=-1, keepdims=True) + EPS)
    o_ref[...] = ((x * r) * w_ref[...] * (1.0 + sc_ref[0]) + sh_ref[0]).astype(o_ref.dtype)


def _lnmod(x2d, w, shift, scale, seq, tm):
    t, d = x2d.shape
    bsz = shift.shape[0]
    per = seq // tm
    return pl.pallas_call(
        _lnmod_kernel,
        grid=(t // tm,),
        in_specs=[pl.BlockSpec((tm, d), lambda i: (i, 0)),
                  pl.BlockSpec((1, d), lambda i: (0, 0)),
                  pl.BlockSpec((1, 1, d), lambda i: (i // per, 0, 0)),
                  pl.BlockSpec((1, 1, d), lambda i: (i // per, 0, 0))],
        out_specs=pl.BlockSpec((tm, d), lambda i: (i, 0)),
        out_shape=jax.ShapeDtypeStruct((t, d), BF16),
        compiler_params=_cparams("parallel"),
        name="lnmod",
    )(x2d, w.reshape(1, d), shift.reshape(bsz, 1, d), scale.reshape(bsz, 1, d))


def _rmsnorm_kernel(x_ref, w_ref, o_ref):
    x = x_ref[...]
    r = lax.rsqrt(jnp.mean(x * x, axis=-1, keepdims=True) + EPS)
    o_ref[...] = (x * r) * w_ref[...]


def _rmsnorm(x2d, w, tm):
    t, d = x2d.shape
    return pl.pallas_call(
        _rmsnorm_kernel,
        grid=(t // tm,),
        in_specs=[pl.BlockSpec((tm, d), lambda i: (i, 0)),
                  pl.BlockSpec((1, d), lambda i: (0, 0))],
        out_specs=pl.BlockSpec((tm, d), lambda i: (i, 0)),
        out_shape=jax.ShapeDtypeStruct((t, d), F32),
        compiler_params=_cparams("parallel"),
        name="final_rmsnorm",
    )(x2d, w.reshape(1, d))


def _mm_kernel(a_ref, w_ref, o_ref):
    o_ref[...] = jnp.dot(a_ref[...], w_ref[...], preferred_element_type=F32).astype(o_ref.dtype)


def _mm(a, w, tm, tn, out_dtype, name):
    m, k = a.shape
    n = w.shape[1]
    return pl.pallas_call(
        _mm_kernel,
        grid=(m // tm, n // tn),
        in_specs=[pl.BlockSpec((tm, k), lambda i, j: (i, 0)),
                  pl.BlockSpec((k, tn), lambda i, j: (0, j))],
        out_specs=pl.BlockSpec((tm, tn), lambda i, j: (i, j)),
        out_shape=jax.ShapeDtypeStruct((m, n), out_dtype),
        compiler_params=_cparams("parallel", "arbitrary"),
        name=name,
    )(a, w)


def _mm_pair_kernel(a_ref, w1_ref, w2_ref, o1_ref, o2_ref):
    a = a_ref[...]
    o1_ref[...] = jnp.dot(a, w1_ref[...], preferred_element_type=F32).astype(o1_ref.dtype)
    o2_ref[...] = jnp.dot(a, w2_ref[...], preferred_element_type=F32).astype(o2_ref.dtype)


def _mm_pair(a, w1, w2, tm, tn, dt1, dt2, name):
    m, k = a.shape
    n = w1.shape[1]
    return pl.pallas_call(
        _mm_pair_kernel,
        grid=(m // tm, n // tn),
        in_specs=[pl.BlockSpec((tm, k), lambda i, j: (i, 0)),
                  pl.BlockSpec((k, tn), lambda i, j: (0, j)),
                  pl.BlockSpec((k, tn), lambda i, j: (0, j))],
        out_specs=[pl.BlockSpec((tm, tn), lambda i, j: (i, j)),
                   pl.BlockSpec((tm, tn), lambda i, j: (i, j))],
        out_shape=[jax.ShapeDtypeStruct((m, n), dt1), jax.ShapeDtypeStruct((m, n), dt2)],
        compiler_params=_cparams("parallel", "arbitrary"),
        name=name,
    )(a, w1, w2)


def _dft_tables(n):
    idx = np.arange(n)
    ang = 2.0 * np.pi * ((idx[:, None] * idx[None, :]) % n) / n
    return np.cos(ang), np.sin(ang)


def _fold_kernel(cs_ref, wf_ref, o_ref):
    o_ref[0] = jnp.dot(cs_ref[...], wf_ref[0], precision=HIGHEST, preferred_element_type=F32)


def _fourier_fold(fnet_w, seq):
    g, c, _ = fnet_w.shape
    cc, sc = _dft_tables(c)
    cs = jnp.asarray(np.concatenate([cc, sc], axis=0) / math.sqrt(seq * c), F32)
    return pl.pallas_call(
        _fold_kernel,
        grid=(g,),
        in_specs=[pl.BlockSpec((2 * c, c), lambda i: (0, 0)),
                  pl.BlockSpec((1, c, c), lambda i: (i, 0, 0))],
        out_specs=pl.BlockSpec((1, 2 * c, c), lambda i: (i, 0, 0)),
        out_shape=jax.ShapeDtypeStruct((g, 2 * c, c), F32),
        compiler_params=_cparams("parallel"),
        name="fnet_fold",
    )(cs, fnet_w)


def _dft1_kernel(f_ref, u_ref, zr_ref, zi_ref):
    n1 = zr_ref.shape[1]
    for j in range(u_ref.shape[2]):
        z = jnp.dot(f_ref[...], u_ref[0, :, j, :], precision=HIGHEST, preferred_element_type=F32)
        zr_ref[0, :, j, :] = z[:n1]
        zi_ref[0, :, j, :] = z[n1:]


def _dft_stage1(u4):
    bsz, n1, n2, wdt = u4.shape
    c1, s1 = _dft_tables(n1)
    f1 = jnp.asarray(np.concatenate([c1, -s1], axis=0), F32)
    spec = pl.BlockSpec((1, n1, SUBLANES, wdt), lambda b, j: (b, 0, j, 0))
    return pl.pallas_call(
        _dft1_kernel,
        grid=(bsz, n2 // SUBLANES),
        in_specs=[pl.BlockSpec((2 * n1, n1), lambda b, j: (0, 0)), spec],
        out_specs=[spec, spec],
        out_shape=[jax.ShapeDtypeStruct(u4.shape, F32)] * 2,
        compiler_params=_cparams("parallel", "parallel"),
        name="fnet_dft_stage1",
    )(f1, u4)


def _dft2_kernel(c2_ref, s2_ref, tc_ref, ts_ref, wcs_ref, zr_ref, zi_ref, o_ref, *, kb, groups, cg):
    c2 = c2_ref[...]
    s2 = s2_ref[...]
    for j in range(kb):
        tcj = tc_ref[j:j + 1, :]
        tsj = ts_ref[j:j + 1, :]
        gc = c2 * tcj - s2 * tsj
        gs = s2 * tcj + c2 * tsj
        zr = zr_ref[0, j]
        zi = zi_ref[0, j]
        dot = functools.partial(jnp.dot, precision=HIGHEST, preferred_element_type=F32)
        pr = dot(gc, zr) + dot(gs, zi)
        pi = dot(gc, zi) - dot(gs, zr)
        for g in range(groups):
            sl = slice(g * cg, (g + 1) * cg)
            w = wcs_ref[g]
            y = dot(pr[:, sl], w[:cg]) + dot(pi[:, sl], w[cg:])
            o_ref[0, :, j, sl] = y.astype(o_ref.dtype)


def _dft_stage2(zr4, zi4, wcs, seq):
    bsz, n1, n2, wdt = zr4.shape
    groups, _, cg = wcs.shape
    c2, s2 = _dft_tables(n2)
    k1 = np.arange(n1)[:, None]
    l2 = np.arange(n2)[None, :]
    ang = 2.0 * np.pi * ((k1 * l2) % seq) / seq
    kb = SUBLANES
    zspec = pl.BlockSpec((1, kb, n2, wdt), lambda b, i: (b, i, 0, 0))
    full = lambda shape: pl.BlockSpec(shape, lambda b, i: (0,) * len(shape))
    return pl.pallas_call(
        functools.partial(_dft2_kernel, kb=kb, groups=groups, cg=cg),
        grid=(bsz, n1 // kb),
        in_specs=[full((n2, n2)), full((n2, n2)),
                  pl.BlockSpec((kb, n2), lambda b, i: (i, 0)),
                  pl.BlockSpec((kb, n2), lambda b, i: (i, 0)),
                  full(wcs.shape), zspec, zspec],
        out_specs=pl.BlockSpec((1, n2, kb, wdt), lambda b, i: (b, 0, i, 0)),
        out_shape=jax.ShapeDtypeStruct((bsz, n2, n1, wdt), BF16),
        compiler_params=_cparams("parallel", "parallel"),
        name="fnet_dft_stage2",
    )(jnp.asarray(c2, F32), jnp.asarray(s2, F32), jnp.asarray(np.cos(ang), F32),
      jnp.asarray(np.sin(ang), F32), wcs, zr4, zi4)


def _fourier_mix(u, fnet_w, bsz, seq):
    wdt = u.shape[1]
    n2 = LANES
    n1 = seq // n2
    wcs = _fourier_fold(fnet_w, seq)
    zr, zi = _dft_stage1(u.reshape(bsz, n1, n2, wdt))
    out = _dft_stage2(zr, zi, wcs, seq)
    return out.reshape(bsz * seq, wdt)


def _dwconv_silu_kernel(prev_ref, main_ref, next_ref, w_ref, b_ref, o_ref, scr, *, per, tl, taps):
    t = pl.program_id(0) % per
    halo = SUBLANES
    scr[0:halo, :] = jnp.where(t == 0, 0.0, prev_ref[...])
    scr[halo:halo + tl, :] = main_ref[...]
    scr[halo + tl:2 * halo + tl, :] = jnp.where(t == per - 1, 0.0, next_ref[...])
    acc = b_ref[...] + w_ref[0:1, :] * scr[pl.ds(halo - (taps - 1) // 2, tl), :]
    for k in range(1, taps):
        acc = acc + w_ref[k:k + 1, :] * scr[pl.ds(halo - (taps - 1) // 2 + k, tl), :]
    o_ref[...] = _silu(acc).astype(o_ref.dtype)


def _dwconv_silu(src, col0, width, conv_w, conv_b, seq, tl, tc):
    t = src.shape[0]
    taps = conv_w.shape[0]
    per = seq // tl
    cb0 = col0 // tc
    hb = tl // SUBLANES
    last = t // SUBLANES - 1
    return pl.pallas_call(
        functools.partial(_dwconv_silu_kernel, per=per, tl=tl, taps=taps),
        grid=(t // tl, width // tc),
        in_specs=[pl.BlockSpec((SUBLANES, tc), lambda i, j: (jnp.maximum(i * hb - 1, 0), cb0 + j)),
                  pl.BlockSpec((tl, tc), lambda i, j: (i, cb0 + j)),
                  pl.BlockSpec((SUBLANES, tc), lambda i, j: (jnp.minimum((i + 1) * hb, last), cb0 + j)),
                  pl.BlockSpec((taps, tc), lambda i, j: (0, j)),
                  pl.BlockSpec((1, tc), lambda i, j: (0, j))],
        out_specs=pl.BlockSpec((tl, tc), lambda i, j: (i, j)),
        out_shape=jax.ShapeDtypeStruct((t, width), F32),
        scratch_shapes=[pltpu.VMEM((tl + 2 * SUBLANES, tc), F32)],
        compiler_params=_cparams("parallel", "parallel"),
        name="ssd_dwconv_silu",
    )(src, src, src, conv_w, conv_b.reshape(1, width))


def _ssd_prep_kernel(raw_ref, bias_ref, alog_ref, dtT_ref, acol_ref, aT_ref, w_ref, *, heads):
    c = SSD_CHUNK
    v = raw_ref[...] + bias_ref[...]
    dt = jnp.maximum(v, 0.0) + jnp.log1p(jnp.exp(-jnp.abs(v)))
    lane = lax.broadcasted_iota(jnp.int32, (c, LANES), 1)
    dt = jnp.where(lane < 2 * heads, dt, 0.0)
    dta = dt * (-jnp.exp(alog_ref[...]))
    li = lax.broadcasted_iota(jnp.int32, (c, c), 0)
    si = lax.broadcasted_iota(jnp.int32, (c, c), 1)
    lower = (si <= li).astype(F32)
    upper = (si >= li).astype(F32)
    pre = jnp.dot(lower, dta, precision=HIGHEST, preferred_element_type=F32)
    suf = jnp.dot(upper, dta, precision=HIGHEST, preferred_element_type=F32)
    acum = jnp.where(lane < heads, pre, suf)
    lane_row = lax.broadcasted_iota(jnp.int32, (1, LANES), 1)
    aend = jnp.where(lane_row < heads, acum[c - 1:c, :], acum[0:1, :])
    dtT_ref[...] = dt.T
    acol_ref[...] = acum
    aT_ref[...] = acum.T
    w_ref[...] = dt * jnp.exp(aend - acum)


def _ssd_prep(src, colblk, dt_bias, a_log, heads):
    t = src.shape[0]
    c = SSD_CHUNK
    pad = LANES - 2 * heads
    bias = jnp.pad(dt_bias.reshape(1, 2 * heads), ((0, 0), (0, pad)))
    alog = jnp.pad(a_log.reshape(1, 2 * heads), ((0, 0), (0, pad)))
    spec = pl.BlockSpec((c, LANES), lambda i: (i, 0))
    vec = pl.BlockSpec((1, LANES), lambda i: (0, 0))
    return pl.pallas_call(
        functools.partial(_ssd_prep_kernel, heads=heads),
        grid=(t // c,),
        in_specs=[pl.BlockSpec((c, LANES), lambda i: (i, colblk)), vec, vec],
        out_specs=[spec] * 4,
        out_shape=[jax.ShapeDtypeStruct((t, LANES), F32)] * 4,
        compiler_params=_cparams("parallel"),
        name="ssd_prep",
    )(src, bias, alog)


def _ssd_scan_kernel(x_ref, b_ref, c_ref, dtT_ref, acol_ref, aT_ref, w_ref, h0_ref, y_ref, hout_ref, h_scr,
                     *, reverse, col0, groups, hpg, nsteps):
    c = SSD_CHUNK
    n = SSD_STATE
    s = pl.program_id(1)

    @pl.when(s == 0)
    def _():
        h_scr[...] = h0_ref[0]

    li = lax.broadcasted_iota(jnp.int32, (c, c), 0)
    si = lax.broadcasted_iota(jnp.int32, (c, c), 1)
    mask = (li <= si) if reverse else (li >= si)
    lo_half = lax.broadcasted_iota(jnp.int32, (c, LANES), 1) < SSD_HEAD_DIM
    lo_half_row = lax.broadcasted_iota(jnp.int32, (1, LANES), 1) < SSD_HEAD_DIM
    acol = acol_ref[...]
    a_t = aT_ref[...]
    dt_t = dtT_ref[...]
    wcol = w_ref[...]
    ecol = jnp.exp(acol)
    eend = ecol[0:1, :] if reverse else ecol[c - 1:c, :]
    pairs = hpg // 2
    for g in range(groups):
        bg = b_ref[:, g * n:(g + 1) * n].astype(BF16)
        cg = c_ref[:, g * n:(g + 1) * n].astype(BF16)
        cb = lax.dot_general(cg, bg, (((1,), (1,)), ((), ())), preferred_element_type=F32)
        h_t = h_scr[g]
        yoff = jnp.dot(cg, h_t.astype(BF16), preferred_element_type=F32)
        for pr in range(pairs):
            lanes = slice((g * pairs + pr) * LANES, (g * pairs + pr + 1) * LANES)
            loc = slice(pr * LANES, (pr + 1) * LANES)
            xp = x_ref[:, lanes]
            xpb = xp.astype(BF16)
            k0 = col0 + g * hpg + 2 * pr
            ys = []
            for q in range(2):
                k = k0 + q
                seg = acol[:, k:k + 1] - a_t[k:k + 1, :]
                dec = jnp.exp(jnp.where(mask, seg, -jnp.inf))
                m = (cb * dec * dt_t[k:k + 1, :]).astype(BF16)
                ys.append(jnp.dot(m, xpb, preferred_element_type=F32))
            e_pair = jnp.where(lo_half, ecol[:, k0:k0 + 1], ecol[:, k0 + 1:k0 + 2])
            y_ref[:, lanes] = jnp.where(lo_half, ys[0], ys[1]) + yoff[:, loc] * e_pair
            w_pair = jnp.where(lo_half, wcol[:, k0:k0 + 1], wcol[:, k0 + 1:k0 + 2])
            xw = (xp * w_pair).astype(BF16)
            upd = lax.dot_general(bg, xw, (((0,), (0,)), ((), ())), preferred_element_type=F32)
            eend_pair = jnp.where(lo_half_row, eend[:, k0:k0 + 1], eend[:, k0 + 1:k0 + 2])
            h_scr[g, :, loc] = h_t[:, loc] * eend_pair + upd

    @pl.when(s == nsteps - 1)
    def _():
        hout_ref[0] = h_scr[...]


def _ssd_scan(act, prep, h0, seq, heads, reverse):
    t = act.shape[0]
    bsz = t // seq
    c, n, groups = SSD_CHUNK, SSD_STATE, SSD_GROUPS
    hpg = heads // groups
    width = heads * SSD_HEAD_DIM
    gn = groups * n
    nsteps = seq // c
    dt_t, acol, a_t, wcol = prep

    def row(b, s):
        return b * nsteps + ((nsteps - 1 - s) if reverse else s)

    small = pl.BlockSpec((c, LANES), lambda b, s: (row(b, s), 0))
    hspec = pl.BlockSpec((1, groups, n, hpg * SSD_HEAD_DIM), lambda b, s: (b, 0, 0, 0))
    return pl.pallas_call(
        functools.partial(_ssd_scan_kernel, reverse=reverse, col0=heads if reverse else 0,
                          groups=groups, hpg=hpg, nsteps=nsteps),
        grid=(bsz, nsteps),
        in_specs=[pl.BlockSpec((c, width), lambda b, s: (row(b, s), 0)),
                  pl.BlockSpec((c, gn), lambda b, s: (row(b, s), width // gn)),
                  pl.BlockSpec((c, gn), lambda b, s: (row(b, s), width // gn + 1)),
                  small, small, small, small, hspec],
        out_specs=[pl.BlockSpec((c, width), lambda b, s: (row(b, s), 0)), hspec],
        out_shape=[jax.ShapeDtypeStruct((t, width), F32),
                   jax.ShapeDtypeStruct(h0.shape, F32)],
        scratch_shapes=[pltpu.VMEM((groups, n, hpg * SSD_HEAD_DIM), F32)],
        compiler_params=_cparams("parallel", "arbitrary"),
        name="ssd_scan_bwd" if reverse else "ssd_scan_fwd",
    )(act, act, act, dt_t, acol, a_t, wcol, h0)


def _ssd_out_kernel(yf_ref, yb_ref, x_ref, z_ref, d_ref, nw_ref, o_ref, *, groups, gw):
    y = yf_ref[...] + yb_ref[...] + d_ref[...] * x_ref[...]
    g = y * _silu(z_ref[...])
    for i in range(groups):
        sl = slice(i * gw, (i + 1) * gw)
        gi = g[:, sl]
        r = lax.rsqrt(jnp.mean(gi * gi, axis=-1, keepdims=True) + EPS)
        o_ref[:, sl] = (gi * r * nw_ref[:, sl]).astype(o_ref.dtype)


def _ssd_out(yf, yb, act, rest, d_exp, norm_w, tm):
    t, width = yf.shape
    gw = width // SSD_GROUPS
    row = pl.BlockSpec((tm, width), lambda i: (i, 0))
    vec = pl.BlockSpec((1, width), lambda i: (0, 0))
    return pl.pallas_call(
        functools.partial(_ssd_out_kernel, groups=SSD_GROUPS, gw=gw),
        grid=(t // tm,),
        in_specs=[row, row, row, row, vec, vec],
        out_specs=row,
        out_shape=jax.ShapeDtypeStruct((t, width), BF16),
        compiler_params=_cparams("parallel"),
        name="ssd_out",
    )(yf, yb, act, rest, d_exp.reshape(1, width), norm_w.reshape(1, width))


def _wout_kernel(f_ref, s_ref, w_ref, x_ref, g_ref, o_ref):
    k1 = f_ref.shape[1]
    acc = jnp.dot(f_ref[...], w_ref[:k1, :], preferred_element_type=F32)
    acc = acc + jnp.dot(s_ref[...], w_ref[k1:, :], preferred_element_type=F32)
    o_ref[...] = x_ref[...] + g_ref[0] * acc


def _wout(f_out, s_out, w, x2d, gate, seq, tm, tn):
    t, d = x2d.shape
    bsz = gate.shape[0]
    k1, k2 = f_out.shape[1], s_out.shape[1]
    per = seq // tm
    return pl.pallas_call(
        _wout_kernel,
        grid=(t // tm, d // tn),
        in_specs=[pl.BlockSpec((tm, k1), lambda i, j: (i, 0)),
                  pl.BlockSpec((tm, k2), lambda i, j: (i, 0)),
                  pl.BlockSpec((k1 + k2, tn), lambda i, j: (0, j)),
                  pl.BlockSpec((tm, tn), lambda i, j: (i, j)),
                  pl.BlockSpec((1, 1, tn), lambda i, j: (i // per, 0, j))],
        out_specs=pl.BlockSpec((tm, tn), lambda i, j: (i, j)),
        out_shape=jax.ShapeDtypeStruct((t, d), F32),
        compiler_params=_cparams("parallel", "arbitrary"),
        name="out_proj",
    )(f_out, s_out, w, x2d, gate.reshape(bsz, 1, d))


def _ffn_down_kernel(up_ref, main_ref, dn_ref, val_ref, cw_ref, cb_ref, w_ref, x_ref, g_ref, nw_ref, o_ref,
                     a0_scr, a1_scr, *, per, nk, nsteps):
    q = pl.program_id(0)
    t = (jnp.minimum(q, nsteps - 1) // nk) % per
    km = jnp.maximum(q - 1, 0) % nk
    gw = GRID_W
    tm, tk = a0_scr.shape

    @pl.when(q == 0)
    def _():
        a1_scr[...] = jnp.zeros((tm, tk), BF16)

    @pl.when(km == 0)
    def _():
        o_ref[...] = jnp.zeros_like(o_ref)

    def conv_and_matmul(a_new, a_prev):
        col = lax.broadcasted_iota(jnp.int32, (gw, LANES), 0)
        not_first = col != 0
        not_last = col != gw - 1
        nrows = tm // gw

        def conv_piece(c, r):
            ls = slice(c * LANES, (c + 1) * LANES)
            cw = cw_ref[:, ls]
            rows = slice(r * gw, (r + 1) * gw)
            if r == 0:
                above = jnp.where(t == 0, 0.0, up_ref[:, ls].astype(F32))
            else:
                above = main_ref[(r - 1) * gw:r * gw, ls].astype(F32)
            mid = main_ref[rows, ls].astype(F32)
            if r == nrows - 1:
                below = jnp.where(t == per - 1, 0.0, dn_ref[:, ls].astype(F32))
            else:
                below = main_ref[(r + 1) * gw:(r + 2) * gw, ls].astype(F32)
            p = [cw[dx:dx + 1] * above + cw[3 + dx:4 + dx] * mid + cw[6 + dx:7 + dx] * below
                 for dx in range(3)]
            left = jnp.where(not_first, pltpu.roll(p[0], 1, 0), 0.0)
            right = jnp.where(not_last, pltpu.roll(p[2], gw - 1, 0), 0.0)
            acc = p[1] + cb_ref[:, ls] + left + right
            a_new[rows, ls] = (_silu(acc) * val_ref[rows, ls].astype(F32)).astype(BF16)

        def matmul_piece(n):
            ns = slice(n * FFN_DOWN_SLAB, (n + 1) * FFN_DOWN_SLAB)
            o_ref[:, ns] += jnp.dot(a_prev[...], w_ref[:, ns], preferred_element_type=F32)

        pieces = [(c, r) for c in range(tk // LANES) for r in range(nrows)]
        nslabs = o_ref.shape[1] // FFN_DOWN_SLAB
        per_slab = -(-len(pieces) // nslabs)
        for n in range(nslabs):
            matmul_piece(n)
            for c, r in pieces[n * per_slab:(n + 1) * per_slab]:
                conv_piece(c, r)

    @pl.when(q % 2 == 0)
    def _():
        conv_and_matmul(a0_scr, a1_scr)

    @pl.when(q % 2 == 1)
    def _():
        conv_and_matmul(a1_scr, a0_scr)

    @pl.when(jnp.logical_and(km == nk - 1, q > 0))
    def _():
        def body(r, carry):
            rows = pl.ds(pl.multiple_of(r * SUBLANES, SUBLANES), SUBLANES)
            v = x_ref[rows, :] + g_ref[0] * o_ref[rows, :]
            s = lax.rsqrt(jnp.mean(v * v, axis=-1, keepdims=True) + EPS)
            o_ref[rows, :] = v * s * nw_ref[...]
            return carry
        lax.fori_loop(0, tm // SUBLANES, body, 0)


def _ffn_down(gate, val, conv_w9, conv_b, w, x2d, g, norm_w, seq, tm, tk):
    t, f = gate.shape
    d = x2d.shape[1]
    bsz = g.shape[0]
    per = seq // tm
    hb = tm // GRID_W
    last = t // GRID_W - 1
    nk = f // tk
    nsteps = (t // tm) * nk

    def conv_tile(q):
        qc = jnp.minimum(q, nsteps - 1)
        return qc // nk, qc % nk

    def mm_tile(q):
        qm = jnp.maximum(q - 1, 0)
        return qm // nk, qm % nk

    def up_map(q):
        i, k = conv_tile(q)
        return jnp.maximum(i * hb - 1, 0), k

    def dn_map(q):
        i, k = conv_tile(q)
        return jnp.minimum((i + 1) * hb, last), k

    return pl.pallas_call(
        functools.partial(_ffn_down_kernel, per=per, nk=nk, nsteps=nsteps),
        grid=(nsteps + 1,),
        in_specs=[pl.BlockSpec((GRID_W, tk), up_map),
                  pl.BlockSpec((tm, tk), conv_tile),
                  pl.BlockSpec((GRID_W, tk), dn_map),
                  pl.BlockSpec((tm, tk), conv_tile),
                  pl.BlockSpec((9, tk), lambda q: (0, conv_tile(q)[1])),
                  pl.BlockSpec((1, tk), lambda q: (0, conv_tile(q)[1])),
                  pl.BlockSpec((tk, d), lambda q: (mm_tile(q)[1], 0)),
                  pl.BlockSpec((tm, d), lambda q: (mm_tile(q)[0], 0)),
                  pl.BlockSpec((1, 1, d), lambda q: (mm_tile(q)[0] // per, 0, 0)),
                  pl.BlockSpec((1, d), lambda q: (0, 0))],
        out_specs=pl.BlockSpec((tm, d), lambda q: (mm_tile(q)[0], 0)),
        out_shape=jax.ShapeDtypeStruct((t, d), F32),
        scratch_shapes=[pltpu.VMEM((tm, tk), BF16), pltpu.VMEM((tm, tk), BF16)],
        compiler_params=_cparams("arbitrary"),
        name="ffn_conv_down",
    )(gate, gate, gate, val, conv_w9, conv_b.reshape(1, f), w, x2d, g.reshape(bsz, 1, d), norm_w.reshape(1, d))


def _tile(n, pref):
    t = pref
    while n % t:
        t //= 2
    return t


def _ssd_branch_inputs(h, w_rest, conv_w, conv_b, dt_bias, a_log, seq, fw, heads, tm):
    width = heads * SSD_HEAD_DIM
    conv_dim = conv_w.shape[1]
    rest = _mm(h, w_rest, tm, 640 if w_rest.shape[1] % 640 == 0 else LANES, F32, "in_proj_ssd")
    act = _dwconv_silu(rest, width, conv_dim, conv_w, conv_b, seq, _tile(seq, 512), 512)
    prep = _ssd_prep(rest, (width + conv_dim) // LANES, dt_bias, a_log, heads)
    return rest, act, prep


def kernel(x, c, ctx, c_ctx, ada_w, ada_b, norm_mix_w, w_in, fnet_w, ssd_conv_w, ssd_conv_b, ssd_dt_bias,
           ssd_a_log, ssd_d, ssd_norm_w, w_out, norm_ffn_w, ffn_w_gate, ffn_w_val, ffn_conv_w, ffn_conv_b,
           ffn_w_down, norm_final_w):
    bsz, seq, d = x.shape
    ctx_len = ctx.shape[1]
    depth = ada_w.shape[0]
    fw = fnet_w.shape[1] * fnet_w.shape[2]
    width = d - fw
    heads = width // SSD_HEAD_DIM
    conv_dim = ssd_conv_w.shape[2]
    assert depth == 1, "only the single-layer block is implemented"
    assert 2 * heads <= LANES and heads % SSD_GROUPS == 0 and (heads // SSD_GROUPS) % 2 == 0
    t = bsz * seq
    layer = 0

    cvecs = jnp.zeros((SUBLANES, d), F32).at[:bsz].set(c).at[bsz].set(c_ctx)
    mod = _adaln(cvecs, ada_w[layer], ada_b[layer])
    sh_m, sc_m, g_m, sh_f, sc_f, g_f = [mod[:bsz, i * d:(i + 1) * d] for i in range(6)]
    csh_m, csc_m = [jnp.broadcast_to(mod[bsz:bsz + 1, i * d:(i + 1) * d], (bsz, d)) for i in range(2)]

    w_in_l = w_in[layer]
    w_u = w_in_l[:, :fw].astype(BF16)
    n_rest = w_in_l.shape[1] - fw
    w_rest = jnp.pad(w_in_l[:, fw:], ((0, 0), (0, -n_rest % LANES))).astype(BF16)
    conv_p = (ssd_conv_w[layer], ssd_conv_b[layer], ssd_dt_bias[layer], ssd_a_log[layer])

    x2d = x.reshape(t, d)
    hc = _lnmod(ctx.reshape(bsz * ctx_len, d), norm_mix_w[layer], csh_m, csc_m, ctx_len, ctx_len)
    _, act_c, prep_c = _ssd_branch_inputs(hc, w_rest, *conv_p, ctx_len, fw, heads, _tile(bsz * ctx_len, 1024))
    h_zero = jnp.zeros((bsz, SSD_GROUPS, SSD_STATE, (heads // SSD_GROUPS) * SSD_HEAD_DIM), F32)
    _, hc_f = _ssd_scan(act_c, prep_c, h_zero, ctx_len, heads, False)
    _, hc_b = _ssd_scan(act_c, prep_c, h_zero, ctx_len, heads, True)

    tm = _tile(seq, 1024)
    tr = _tile(seq, 256)
    hx = _lnmod(x2d, norm_mix_w[layer], sh_m, sc_m, seq, tr)
    u = _mm(hx, w_u, tm, 512, F32, "in_proj_fnet")
    f_out = _fourier_mix(u, fnet_w[layer], bsz, seq)
    rest, act, prep = _ssd_branch_inputs(hx, w_rest, *conv_p, seq, fw, heads, tm)
    y_f, _ = _ssd_scan(act, prep, hc_f, seq, heads, False)
    y_b, _ = _ssd_scan(act, prep, hc_b, seq, heads, True)
    d_exp = jnp.repeat(ssd_d[layer], SSD_HEAD_DIM)
    s_out = _ssd_out(y_f, y_b, act, rest, d_exp, ssd_norm_w[layer], tr)
    x1 = _wout(f_out, s_out, w_out[layer].astype(BF16), x2d, g_m, seq, tm, 512)

    hx2 = _lnmod(x1, norm_ffn_w[layer], sh_f, sc_f, seq, tr)
    f = ffn_w_gate.shape[2]
    fpad = -f % FFN_TILE
    pad_cols = lambda w: jnp.pad(w, ((0, 0), (0, fpad)))
    gate, val = _mm_pair(hx2, pad_cols(ffn_w_gate[layer]).astype(BF16), pad_cols(ffn_w_val[layer]).astype(BF16),
                         tm, FFN_TILE, BF16, BF16, "ffn_up")
    w_down = jnp.pad(ffn_w_down[layer], ((0, fpad), (0, 0))).astype(BF16)
    out = _ffn_down(gate, val, pad_cols(ffn_conv_w[layer].reshape(9, f)), jnp.pad(ffn_conv_b[layer], (0, fpad)),
                    w_down, x1, g_f, norm_final_w, seq, _tile(seq, 512), FFN_TILE)
    return out.reshape(bsz, seq, d)
```

```python
import functools
import math

import numpy as np
import jax
import jax.numpy as jnp
from jax import lax
from jax.experimental import pallas as pl
from jax.experimental.pallas import tpu as pltpu

F32 = jnp.float32
BF16 = jnp.bfloat16
HIGHEST = lax.Precision.HIGHEST

GRID_W = 64
SSD_HEAD_DIM = 64
SSD_GROUPS = 8
SSD_STATE = 128
SSD_CHUNK = 128
EPS = 1e-6

LANES = 128
SUBLANES = 8
MXU_WIDTH = 256
IN_PROJ_TILE = 768
VMEM_LIMIT = 56 * 1024 * 1024
FFN_TILE = 512
FFN_DOWN_SLAB = 512


def _cparams(*sem):
    return pltpu.CompilerParams(dimension_semantics=sem, vmem_limit_bytes=VMEM_LIMIT)


def _silu(v):
    return v * jax.nn.sigmoid(v)


def _adaln_kernel(c_ref, w_ref, b_ref, o_ref):
    c = c_ref[...]
    s = _silu(c).astype(BF16)
    o_ref[...] = jnp.dot(s, w_ref[...].astype(BF16), preferred_element_type=F32) + b_ref[...]


def _adaln(cvecs, w, b):
    d, n = w.shape
    tn = 512
    return pl.pallas_call(
        _adaln_kernel,
        grid=(n // tn,),
        in_specs=[pl.BlockSpec((SUBLANES, d), lambda j: (0, 0)),
                  pl.BlockSpec((d, tn), lambda j: (0, j)),
                  pl.BlockSpec((1, tn), lambda j: (0, j))],
        out_specs=pl.BlockSpec((SUBLANES, tn), lambda j: (0, j)),
        out_shape=jax.ShapeDtypeStruct((SUBLANES, n), F32),
        compiler_params=_cparams("parallel"),
        name="adaln",
    )(cvecs, w, b.reshape(1, n))


def _lnmod_kernel(x_ref, w_ref, sh_ref, sc_ref, o_ref):
    x = x_ref[...]
    r = lax.rsqrt(jnp.mean(x * x, axis=-1, keepdims=True) + EPS)
    o_ref[...] = ((x * r) * w_ref[...] * (1.0 + sc_ref[0]) + sh_ref[0]).astype(o_ref.dtype)


def _lnmod(x2d, w, shift, scale, seq, tm):
    t, d = x2d.shape
    bsz = shift.shape[0]
    per = seq // tm
    return pl.pallas_call(
        _lnmod_kernel,
        grid=(t // tm,),
        in_specs=[pl.BlockSpec((tm, d), lambda i: (i, 0)),
                  pl.BlockSpec((1, d), lambda i: (0, 0)),
                  pl.BlockSpec((1, 1, d), lambda i: (i // per, 0, 0)),
                  pl.BlockSpec((1, 1, d), lambda i: (i // per, 0, 0))],
        out_specs=pl.BlockSpec((tm, d), lambda i: (i, 0)),
        out_shape=jax.ShapeDtypeStruct((t, d), BF16),
        compiler_params=_cparams("parallel"),
        name="lnmod",
    )(x2d, w.reshape(1, d), shift.reshape(bsz, 1, d), scale.reshape(bsz, 1, d))


def _cast_pad_kernel(w_ref, o_ref, *, rows, cols, tr, tc):
    r = lax.broadcasted_iota(jnp.int32, (tr, tc), 0) + pl.program_id(0) * tr
    c = lax.broadcasted_iota(jnp.int32, (tr, tc), 1) + pl.program_id(1) * tc
    o_ref[...] = jnp.where(jnp.logical_and(r < rows, c < cols), w_ref[...], 0.0).astype(o_ref.dtype)


def _cast_pad(w, col0, cols, out_rows, out_cols, tr, tc):
    rows = w.shape[0]
    assert col0 % tc == 0 and out_rows % tr == 0 and out_cols % tc == 0
    cb0 = col0 // tc
    return pl.pallas_call(
        functools.partial(_cast_pad_kernel, rows=rows, cols=cols, tr=tr, tc=tc),
        grid=(out_rows // tr, out_cols // tc),
        in_specs=[pl.BlockSpec((tr, tc), lambda i, j: (i, cb0 + j))],
        out_specs=pl.BlockSpec((tr, tc), lambda i, j: (i, j)),
        out_shape=jax.ShapeDtypeStruct((out_rows, out_cols), BF16),
        compiler_params=_cparams("parallel", "parallel"),
        name="weight_cast",
    )(w)


def _mm_kernel(a_ref, w_ref, o_ref):
    o_ref[...] = jnp.dot(a_ref[...], w_ref[...], preferred_element_type=F32).astype(o_ref.dtype)


def _mm(a, w, tm, tn, out_dtype, name):
    m, k = a.shape
    n = w.shape[1]
    return pl.pallas_call(
        _mm_kernel,
        grid=(m // tm, n // tn),
        in_specs=[pl.BlockSpec((tm, k), lambda i, j: (i, 0)),
                  pl.BlockSpec((k, tn), lambda i, j: (0, j))],
        out_specs=pl.BlockSpec((tm, tn), lambda i, j: (i, j)),
        out_shape=jax.ShapeDtypeStruct((m, n), out_dtype),
        compiler_params=_cparams("parallel", "arbitrary"),
        name=name,
    )(a, w)


def _mm_pair_kernel(a_ref, w1_ref, w2_ref, o1_ref, o2_ref):
    a = a_ref[...]
    o1_ref[...] = jnp.dot(a, w1_ref[...], preferred_element_type=F32).astype(o1_ref.dtype)
    o2_ref[...] = jnp.dot(a, w2_ref[...], preferred_element_type=F32).astype(o2_ref.dtype)


def _mm_pair(a, w1, w2, tm, tn, dt1, dt2, name):
    m, k = a.shape
    n = w1.shape[1]
    return pl.pallas_call(
        _mm_pair_kernel,
        grid=(m // tm, n // tn),
        in_specs=[pl.BlockSpec((tm, k), lambda i, j: (i, 0)),
                  pl.BlockSpec((k, tn), lambda i, j: (0, j)),
                  pl.BlockSpec((k, tn), lambda i, j: (0, j))],
        out_specs=[pl.BlockSpec((tm, tn), lambda i, j: (i, j)),
                   pl.BlockSpec((tm, tn), lambda i, j: (i, j))],
        out_shape=[jax.ShapeDtypeStruct((m, n), dt1), jax.ShapeDtypeStruct((m, n), dt2)],
        compiler_params=_cparams("parallel", "arbitrary"),
        name=name,
    )(a, w1, w2)


def _dft_tables(n):
    idx = np.arange(n)
    ang = 2.0 * np.pi * ((idx[:, None] * idx[None, :]) % n) / n
    return np.cos(ang), np.sin(ang)


def _fold_kernel(cs_ref, wf_ref, o_ref):
    o_ref[0] = jnp.dot(cs_ref[...], wf_ref[0], precision=HIGHEST,
                       preferred_element_type=F32).astype(o_ref.dtype)


def _fourier_fold(fnet_w, seq):
    g, c, _ = fnet_w.shape
    cc, sc = _dft_tables(c)
    cs = jnp.asarray(np.concatenate([cc, sc], axis=0) / math.sqrt(seq * c), F32)
    return pl.pallas_call(
        _fold_kernel,
        grid=(g,),
        in_specs=[pl.BlockSpec((2 * c, c), lambda i: (0, 0)),
                  pl.BlockSpec((1, c, c), lambda i: (i, 0, 0))],
        out_specs=pl.BlockSpec((1, 2 * c, c), lambda i: (i, 0, 0)),
        out_shape=jax.ShapeDtypeStruct((g, 2 * c, c), BF16),
        compiler_params=_cparams("parallel"),
        name="fnet_fold",
    )(cs, fnet_w)


def _dft1_kernel(f_ref, u_ref, zr_ref, zi_ref):
    n1 = zr_ref.shape[1]
    f = f_ref[...]
    for j in range(u_ref.shape[2]):
        z = jnp.dot(f, u_ref[0, :, j, :].astype(BF16), preferred_element_type=F32)
        zr_ref[0, :, j, :] = z[:n1]
        zi_ref[0, :, j, :] = z[n1:]


def _dft_stage1(u4):
    bsz, n1, n2, wdt = u4.shape
    c1, s1 = _dft_tables(n1)
    f1 = jnp.asarray(np.concatenate([c1, -s1], axis=0), BF16)
    spec = pl.BlockSpec((1, n1, SUBLANES, wdt), lambda b, j: (b, 0, j, 0))
    return pl.pallas_call(
        _dft1_kernel,
        grid=(bsz, n2 // SUBLANES),
        in_specs=[pl.BlockSpec((2 * n1, n1), lambda b, j: (0, 0)), spec],
        out_specs=[spec, spec],
        out_shape=[jax.ShapeDtypeStruct(u4.shape, F32)] * 2,
        compiler_params=_cparams("parallel", "parallel"),
        name="fnet_dft_stage1",
    )(f1, u4)


def _dft2_kernel(c2_ref, s2_ref, tc_ref, ts_ref, wcs_ref, zr_ref, zi_ref, o_ref, *, kb, groups, cg):
    c2 = c2_ref[...]
    s2 = s2_ref[...]
    for j in range(kb):
        tcj = tc_ref[j:j + 1, :]
        tsj = ts_ref[j:j + 1, :]
        gc = c2 * tcj - s2 * tsj
        gs = s2 * tcj + c2 * tsj
        gmat = jnp.concatenate([jnp.concatenate([gc, gs], axis=1),
                                jnp.concatenate([-gs, gc], axis=1)], axis=0).astype(BF16)
        z = jnp.concatenate([zr_ref[0, j], zi_ref[0, j]], axis=0).astype(BF16)
        p = jnp.dot(gmat, z, preferred_element_type=F32)
        n2 = c2.shape[0]
        pr = p[:n2].astype(BF16)
        pi = p[n2:].astype(BF16)
        for g in range(groups):
            sl = slice(g * cg, (g + 1) * cg)
            y = jnp.dot(jnp.concatenate([pr[:, sl], pi[:, sl]], axis=1), wcs_ref[g],
                        preferred_element_type=F32)
            o_ref[0, :, j, sl] = y.astype(o_ref.dtype)


def _dft_stage2(zr4, zi4, wcs, seq):
    bsz, n1, n2, wdt = zr4.shape
    groups, _, cg = wcs.shape
    c2, s2 = _dft_tables(n2)
    k1 = np.arange(n1)[:, None]
    l2 = np.arange(n2)[None, :]
    ang = 2.0 * np.pi * ((k1 * l2) % seq) / seq
    kb = SUBLANES
    zspec = pl.BlockSpec((1, kb, n2, wdt), lambda b, i: (b, i, 0, 0))
    full = lambda shape: pl.BlockSpec(shape, lambda b, i: (0,) * len(shape))
    return pl.pallas_call(
        functools.partial(_dft2_kernel, kb=kb, groups=groups, cg=cg),
        grid=(bsz, n1 // kb),
        in_specs=[full((n2, n2)), full((n2, n2)),
                  pl.BlockSpec((kb, n2), lambda b, i: (i, 0)),
                  pl.BlockSpec((kb, n2), lambda b, i: (i, 0)),
                  full(wcs.shape), zspec, zspec],
        out_specs=pl.BlockSpec((1, n2, kb, wdt), lambda b, i: (b, 0, i, 0)),
        out_shape=jax.ShapeDtypeStruct((bsz, n2, n1, wdt), BF16),
        compiler_params=_cparams("parallel", "parallel"),
        name="fnet_dft_stage2",
    )(jnp.asarray(c2, F32), jnp.asarray(s2, F32), jnp.asarray(np.cos(ang), F32),
      jnp.asarray(np.sin(ang), F32), wcs, zr4, zi4)


def _fourier_mix(u, fnet_w, bsz, seq):
    wdt = u.shape[1]
    n2 = LANES
    n1 = seq // n2
    wcs = _fourier_fold(fnet_w, seq)
    zr, zi = _dft_stage1(u.reshape(bsz, n1, n2, wdt))
    out = _dft_stage2(zr, zi, wcs, seq)
    return out.reshape(bsz * seq, wdt)


def _dwconv_silu_kernel(prev_ref, main_ref, next_ref, w_ref, b_ref, o_ref, scr, *, per, tl, taps):
    t = pl.program_id(0) % per
    halo = SUBLANES
    scr[0:halo, :] = jnp.where(t == 0, 0.0, prev_ref[...])
    scr[halo:halo + tl, :] = main_ref[...]
    scr[halo + tl:2 * halo + tl, :] = jnp.where(t == per - 1, 0.0, next_ref[...])
    acc = b_ref[...] + w_ref[0:1, :] * scr[pl.ds(halo - (taps - 1) // 2, tl), :]
    for k in range(1, taps):
        acc = acc + w_ref[k:k + 1, :] * scr[pl.ds(halo - (taps - 1) // 2 + k, tl), :]
    o_ref[...] = _silu(acc).astype(o_ref.dtype)


def _dwconv_silu(src, col0, width, conv_w, conv_b, seq, tl, tc):
    t = src.shape[0]
    taps = conv_w.shape[0]
    per = seq // tl
    cb0 = col0 // tc
    hb = tl // SUBLANES
    last = t // SUBLANES - 1
    return pl.pallas_call(
        functools.partial(_dwconv_silu_kernel, per=per, tl=tl, taps=taps),
        grid=(t // tl, width // tc),
        in_specs=[pl.BlockSpec((SUBLANES, tc), lambda i, j: (jnp.maximum(i * hb - 1, 0), cb0 + j)),
                  pl.BlockSpec((tl, tc), lambda i, j: (i, cb0 + j)),
                  pl.BlockSpec((SUBLANES, tc), lambda i, j: (jnp.minimum((i + 1) * hb, last), cb0 + j)),
                  pl.BlockSpec((taps, tc), lambda i, j: (0, j)),
                  pl.BlockSpec((1, tc), lambda i, j: (0, j))],
        out_specs=pl.BlockSpec((tl, tc), lambda i, j: (i, j)),
        out_shape=jax.ShapeDtypeStruct((t, width), BF16),
        scratch_shapes=[pltpu.VMEM((tl + 2 * SUBLANES, tc), F32)],
        compiler_params=_cparams("parallel", "parallel"),
        name="ssd_dwconv_silu",
    )(src, src, src, conv_w, conv_b.reshape(1, width))


def _ssd_prep_kernel(raw_ref, bias_ref, alog_ref, dtT_ref, acol_ref, aT_ref, w_ref, *, heads):
    c = SSD_CHUNK
    v = raw_ref[...] + bias_ref[...]
    dt = jnp.maximum(v, 0.0) + jnp.log1p(jnp.exp(-jnp.abs(v)))
    lane = lax.broadcasted_iota(jnp.int32, (c, LANES), 1)
    dt = jnp.where(lane < 2 * heads, dt, 0.0)
    dta = dt * (-jnp.exp(alog_ref[...]))
    li = lax.broadcasted_iota(jnp.int32, (c, c), 0)
    si = lax.broadcasted_iota(jnp.int32, (c, c), 1)
    lower = (si <= li).astype(F32)
    upper = (si >= li).astype(F32)
    pre = jnp.dot(lower, dta, precision=HIGHEST, preferred_element_type=F32)
    suf = jnp.dot(upper, dta, precision=HIGHEST, preferred_element_type=F32)
    acum = jnp.where(lane < heads, pre, suf)
    lane_row = lax.broadcasted_iota(jnp.int32, (1, LANES), 1)
    aend = jnp.where(lane_row < heads, acum[c - 1:c, :], acum[0:1, :])
    dtT_ref[...] = dt.T
    acol_ref[...] = acum
    aT_ref[...] = acum.T
    w_ref[...] = (dt * jnp.exp(aend - acum)).T


def _ssd_prep(src, colblk, dt_bias, a_log, heads):
    t = src.shape[0]
    c = SSD_CHUNK
    pad = LANES - 2 * heads
    bias = jnp.pad(dt_bias.reshape(1, 2 * heads), ((0, 0), (0, pad)))
    alog = jnp.pad(a_log.reshape(1, 2 * heads), ((0, 0), (0, pad)))
    spec = pl.BlockSpec((c, LANES), lambda i: (i, 0))
    vec = pl.BlockSpec((1, LANES), lambda i: (0, 0))
    return pl.pallas_call(
        functools.partial(_ssd_prep_kernel, heads=heads),
        grid=(t // c,),
        in_specs=[pl.BlockSpec((c, LANES), lambda i: (i, colblk)), vec, vec],
        out_specs=[spec] * 4,
        out_shape=[jax.ShapeDtypeStruct((t, LANES), F32)] * 4,
        compiler_params=_cparams("parallel"),
        name="ssd_prep",
    )(src, bias, alog)


def _ssd_scan_kernel(x_ref, b_ref, c_ref, dtT_ref, acol_ref, aT_ref, w_ref, h0_ref, y_ref, hout_ref, h_scr,
                     *, reverse, col0, groups, hpg, nsteps):
    c = SSD_CHUNK
    n = SSD_STATE
    p = SSD_HEAD_DIM
    bsz = x_ref.shape[0]
    s = pl.program_id(0)

    @pl.when(s == 0)
    def _():
        h_scr[...] = h0_ref[...]

    li = lax.broadcasted_iota(jnp.int32, (c, c), 0)
    si = lax.broadcasted_iota(jnp.int32, (c, c), 1)
    mask = (li <= si) if reverse else (li >= si)
    lo_half = lax.broadcasted_iota(jnp.int32, (c, LANES), 1) < p
    pairs = hpg // 2
    nt = (((1,), (1,)), ((), ()))

    def stream(b):
        acol = acol_ref[b]
        a_t = aT_ref[b]
        dt_t = dtT_ref[b]
        w_t = w_ref[b]
        eend = jnp.exp(acol[0:1, :] if reverse else acol[c - 1:c, :])
        shared = {}

        def group_piece(g):
            bg = b_ref[b, :, g * n:(g + 1) * n]
            cg = c_ref[b, :, g * n:(g + 1) * n]
            h = h_scr[b, g]
            shared[g] = (bg,
                         lax.dot_general(cg, bg, nt, preferred_element_type=F32),
                         h,
                         lax.dot_general(cg, h.astype(BF16), nt, preferred_element_type=F32))

        def pair_piece(g, pr):
            bg, cb, h, yoff = shared[g]
            lanes = slice((g * pairs + pr) * LANES, (g * pairs + pr + 1) * LANES)
            loc = slice(pr * LANES, (pr + 1) * LANES)
            xpb = x_ref[b, :, lanes]
            k0 = col0 + g * hpg + 2 * pr
            ys, abc = [], []
            for q in range(2):
                k = k0 + q
                abc.append(jnp.broadcast_to(acol[:, k:k + 1], (c, c)))
                dec = jnp.exp(jnp.where(mask, abc[q] - a_t[k:k + 1, :], -jnp.inf))
                m = (cb * dec * dt_t[k:k + 1, :]).astype(BF16)
                ys.append(jnp.dot(m, xpb, preferred_element_type=F32))
            e_pair = jnp.exp(jnp.where(lo_half, abc[0], abc[1]))
            y = jnp.where(lo_half, ys[0], ys[1]) + yoff[:, loc] * e_pair
            y_ref[b, :, lanes] = y.astype(y_ref.dtype)
            per_head = lambda v0, v1, width: jnp.concatenate(
                [jnp.broadcast_to(v0, (p, width)), jnp.broadcast_to(v1, (p, width))], axis=0)
            xw_t = (xpb.astype(F32).T * per_head(w_t[k0:k0 + 1, :], w_t[k0 + 1:k0 + 2, :], c)).astype(BF16)
            upd = jnp.dot(xw_t, bg, preferred_element_type=F32)
            h_scr[b, g, loc, :] = h[loc] * per_head(eend[:, k0:k0 + 1], eend[:, k0 + 1:k0 + 2], n) + upd

        pieces = []
        for g in range(groups):
            pieces.append(functools.partial(group_piece, g))
            pieces.extend(functools.partial(pair_piece, g, pr) for pr in range(pairs))
        return pieces

    for step in zip(*[stream(b) for b in range(bsz)]):
        for piece in step:
            piece()

    @pl.when(s == nsteps - 1)
    def _():
        hout_ref[...] = h_scr[...]


def _ssd_scan(act, prep, h0, seq, heads, reverse):
    t = act.shape[0]
    bsz = t // seq
    c, n, groups = SSD_CHUNK, SSD_STATE, SSD_GROUPS
    hpg = heads // groups
    width = heads * SSD_HEAD_DIM
    gn = groups * n
    nsteps = seq // c
    act3 = act.reshape(bsz, seq, act.shape[1])
    prep3 = [p.reshape(bsz, seq, LANES) for p in prep]

    def chunk(s):
        return (nsteps - 1 - s) if reverse else s

    small = pl.BlockSpec((bsz, c, LANES), lambda s: (0, chunk(s), 0))
    hspec = pl.BlockSpec(h0.shape, lambda s: (0, 0, 0, 0))
    y, h = pl.pallas_call(
        functools.partial(_ssd_scan_kernel, reverse=reverse, col0=heads if reverse else 0,
                          groups=groups, hpg=hpg, nsteps=nsteps),
        grid=(nsteps,),
        in_specs=[pl.BlockSpec((bsz, c, width), lambda s: (0, chunk(s), 0)),
                  pl.BlockSpec((bsz, c, gn), lambda s: (0, chunk(s), width // gn)),
                  pl.BlockSpec((bsz, c, gn), lambda s: (0, chunk(s), width // gn + 1)),
                  small, small, small, small, hspec],
        out_specs=[pl.BlockSpec((bsz, c, width), lambda s: (0, chunk(s), 0)), hspec],
        out_shape=[jax.ShapeDtypeStruct((bsz, seq, width), BF16),
                   jax.ShapeDtypeStruct(h0.shape, F32)],
        scratch_shapes=[pltpu.VMEM(h0.shape, F32)],
        compiler_params=_cparams("arbitrary"),
        name="ssd_scan_bwd" if reverse else "ssd_scan_fwd",
    )(act3, act3, act3, *prep3, h0)
    return y.reshape(t, width), h


def _ssd_out_kernel(yf_ref, yb_ref, x_ref, z_ref, d_ref, nw_ref, o_ref, *, groups, gw):
    y = yf_ref[...].astype(F32) + yb_ref[...].astype(F32) + d_ref[...] * x_ref[...].astype(F32)
    g = y * _silu(z_ref[...])
    for i in range(groups):
        sl = slice(i * gw, (i + 1) * gw)
        gi = g[:, sl]
        r = lax.rsqrt(jnp.mean(gi * gi, axis=-1, keepdims=True) + EPS)
        o_ref[:, sl] = (gi * r * nw_ref[:, sl]).astype(o_ref.dtype)


def _ssd_out(yf, yb, act, rest, d_exp, norm_w, tm):
    t, width = yf.shape
    gw = width // SSD_GROUPS
    row = pl.BlockSpec((tm, width), lambda i: (i, 0))
    vec = pl.BlockSpec((1, width), lambda i: (0, 0))
    return pl.pallas_call(
        functools.partial(_ssd_out_kernel, groups=SSD_GROUPS, gw=gw),
        grid=(t // tm,),
        in_specs=[row, row, row, row, vec, vec],
        out_specs=row,
        out_shape=jax.ShapeDtypeStruct((t, width), BF16),
        compiler_params=_cparams("parallel"),
        name="ssd_out",
    )(yf, yb, act, rest, d_exp.reshape(1, width), norm_w.reshape(1, width))


def _wout_kernel(f_ref, s_ref, w_ref, x_ref, g_ref, o_ref):
    k1 = f_ref.shape[1]
    acc = jnp.dot(f_ref[...], w_ref[:k1, :], preferred_element_type=F32)
    acc = acc + jnp.dot(s_ref[...], w_ref[k1:, :], preferred_element_type=F32)
    o_ref[...] = x_ref[...] + g_ref[0] * acc


def _wout(f_out, s_out, w, x2d, gate, seq, tm, tn):
    t, d = x2d.shape
    bsz = gate.shape[0]
    k1, k2 = f_out.shape[1], s_out.shape[1]
    per = seq // tm
    return pl.pallas_call(
        _wout_kernel,
        grid=(t // tm, d // tn),
        in_specs=[pl.BlockSpec((tm, k1), lambda i, j: (i, 0)),
                  pl.BlockSpec((tm, k2), lambda i, j: (i, 0)),
                  pl.BlockSpec((k1 + k2, tn), lambda i, j: (0, j)),
                  pl.BlockSpec((tm, tn), lambda i, j: (i, j)),
                  pl.BlockSpec((1, 1, tn), lambda i, j: (i // per, 0, j))],
        out_specs=pl.BlockSpec((tm, tn), lambda i, j: (i, j)),
        out_shape=jax.ShapeDtypeStruct((t, d), F32),
        compiler_params=_cparams("parallel", "arbitrary"),
        name="out_proj",
    )(f_out, s_out, w, x2d, gate.reshape(bsz, 1, d))


def _ffn_down_kernel(up_ref, main_ref, dn_ref, val_ref, cw_ref, cb_ref, w_ref, x_ref, g_ref, nw_ref, o_ref,
                     a0_scr, a1_scr, *, per, nk, nsteps):
    q = pl.program_id(0)
    t = (jnp.minimum(q, nsteps - 1) // nk) % per
    km = jnp.maximum(q - 1, 0) % nk
    gw = GRID_W
    tm, tk = a0_scr.shape

    @pl.when(q == 0)
    def _():
        a1_scr[...] = jnp.zeros((tm, tk), BF16)

    @pl.when(km == 0)
    def _():
        o_ref[...] = jnp.zeros_like(o_ref)

    def conv_and_matmul(a_new, a_prev):
        col = lax.broadcasted_iota(jnp.int32, (gw, LANES), 0)
        not_first = col != 0
        not_last = col != gw - 1
        nrows = tm // gw

        def conv_piece(c, r):
            ls = slice(c * LANES, (c + 1) * LANES)
            cw = cw_ref[:, ls]
            rows = slice(r * gw, (r + 1) * gw)
            if r == 0:
                above = jnp.where(t == 0, 0.0, up_ref[:, ls].astype(F32))
            else:
                above = main_ref[(r - 1) * gw:r * gw, ls].astype(F32)
            mid = main_ref[rows, ls].astype(F32)
            if r == nrows - 1:
                below = jnp.where(t == per - 1, 0.0, dn_ref[:, ls].astype(F32))
            else:
                below = main_ref[(r + 1) * gw:(r + 2) * gw, ls].astype(F32)
            p = [cw[dx:dx + 1] * above + cw[3 + dx:4 + dx] * mid + cw[6 + dx:7 + dx] * below
                 for dx in range(3)]
            left = jnp.where(not_first, pltpu.roll(p[0], 1, 0), 0.0)
            right = jnp.where(not_last, pltpu.roll(p[2], gw - 1, 0), 0.0)
            acc = p[1] + cb_ref[:, ls] + left + right
            a_new[rows, ls] = (_silu(acc) * val_ref[rows, ls].astype(F32)).astype(BF16)

        def matmul_piece(n):
            ns = slice(n * FFN_DOWN_SLAB, (n + 1) * FFN_DOWN_SLAB)
            o_ref[:, ns] += jnp.dot(a_prev[...], w_ref[:, ns], preferred_element_type=F32)

        pieces = [(c, r) for c in range(tk // LANES) for r in range(nrows)]
        nslabs = o_ref.shape[1] // FFN_DOWN_SLAB
        per_slab = -(-len(pieces) // nslabs)
        for n in range(nslabs):
            matmul_piece(n)
            for c, r in pieces[n * per_slab:(n + 1) * per_slab]:
                conv_piece(c, r)

    @pl.when(q % 2 == 0)
    def _():
        conv_and_matmul(a0_scr, a1_scr)

    @pl.when(q % 2 == 1)
    def _():
        conv_and_matmul(a1_scr, a0_scr)

    @pl.when(jnp.logical_and(km == nk - 1, q > 0))
    def _():
        def body(r, carry):
            rows = pl.ds(pl.multiple_of(r * SUBLANES, SUBLANES), SUBLANES)
            v = x_ref[rows, :] + g_ref[0] * o_ref[rows, :]
            s = lax.rsqrt(jnp.mean(v * v, axis=-1, keepdims=True) + EPS)
            o_ref[rows, :] = v * s * nw_ref[...]
            return carry
        lax.fori_loop(0, tm // SUBLANES, body, 0)


def _ffn_down(gate, val, conv_w9, conv_b, w, x2d, g, norm_w, seq, tm, tk):
    t, f = gate.shape
    d = x2d.shape[1]
    bsz = g.shape[0]
    per = seq // tm
    hb = tm // GRID_W
    last = t // GRID_W - 1
    nk = f // tk
    nsteps = (t // tm) * nk

    def conv_tile(q):
        qc = jnp.minimum(q, nsteps - 1)
        return qc // nk, qc % nk

    def mm_tile(q):
        qm = jnp.maximum(q - 1, 0)
        return qm // nk, qm % nk

    def up_map(q):
        i, k = conv_tile(q)
        return jnp.maximum(i * hb - 1, 0), k

    def dn_map(q):
        i, k = conv_tile(q)
        return jnp.minimum((i + 1) * hb, last), k

    return pl.pallas_call(
        functools.partial(_ffn_down_kernel, per=per, nk=nk, nsteps=nsteps),
        grid=(nsteps + 1,),
        in_specs=[pl.BlockSpec((GRID_W, tk), up_map),
                  pl.BlockSpec((tm, tk), conv_tile),
                  pl.BlockSpec((GRID_W, tk), dn_map),
                  pl.BlockSpec((tm, tk), conv_tile),
                  pl.BlockSpec((9, tk), lambda q: (0, conv_tile(q)[1])),
                  pl.BlockSpec((1, tk), lambda q: (0, conv_tile(q)[1])),
                  pl.BlockSpec((tk, d), lambda q: (mm_tile(q)[1], 0)),
                  pl.BlockSpec((tm, d), lambda q: (mm_tile(q)[0], 0)),
                  pl.BlockSpec((1, 1, d), lambda q: (mm_tile(q)[0] // per, 0, 0)),
                  pl.BlockSpec((1, d), lambda q: (0, 0))],
        out_specs=pl.BlockSpec((tm, d), lambda q: (mm_tile(q)[0], 0)),
        out_shape=jax.ShapeDtypeStruct((t, d), F32),
        scratch_shapes=[pltpu.VMEM((tm, tk), BF16), pltpu.VMEM((tm, tk), BF16)],
        compiler_params=_cparams("arbitrary"),
        name="ffn_conv_down",
    )(gate, gate, gate, val, conv_w9, conv_b.reshape(1, f), w, x2d, g.reshape(bsz, 1, d), norm_w.reshape(1, d))


def _tile(n, pref):
    t = pref
    while n % t:
        t //= 2
    return t


def _ssd_branch_inputs(h, w_rest, conv_w, conv_b, dt_bias, a_log, seq, fw, heads, tm):
    width = heads * SSD_HEAD_DIM
    conv_dim = conv_w.shape[1]
    rest = _mm(h, w_rest, tm, IN_PROJ_TILE, F32, "in_proj_ssd")
    act = _dwconv_silu(rest, width, conv_dim, conv_w, conv_b, seq, _tile(seq, 512), 512)
    prep = _ssd_prep(rest, (width + conv_dim) // LANES, dt_bias, a_log, heads)
    return rest, act, prep


def kernel(x, c, ctx, c_ctx, ada_w, ada_b, norm_mix_w, w_in, fnet_w, ssd_conv_w, ssd_conv_b, ssd_dt_bias,
           ssd_a_log, ssd_d, ssd_norm_w, w_out, norm_ffn_w, ffn_w_gate, ffn_w_val, ffn_conv_w, ffn_conv_b,
           ffn_w_down, norm_final_w):
    bsz, seq, d = x.shape
    ctx_len = ctx.shape[1]
    depth = ada_w.shape[0]
    fw = fnet_w.shape[1] * fnet_w.shape[2]
    width = d - fw
    heads = width // SSD_HEAD_DIM
    conv_dim = ssd_conv_w.shape[2]
    assert depth == 1, "only the single-layer block is implemented"
    assert 2 * heads <= LANES and heads % SSD_GROUPS == 0 and (heads // SSD_GROUPS) % 2 == 0
    t = bsz * seq
    layer = 0

    cvecs = jnp.zeros((SUBLANES, d), F32).at[:bsz].set(c).at[bsz].set(c_ctx)
    mod = _adaln(cvecs, ada_w[layer], ada_b[layer])
    sh_m, sc_m, g_m, sh_f, sc_f, g_f = [mod[:bsz, i * d:(i + 1) * d] for i in range(6)]
    csh_m, csc_m = [jnp.broadcast_to(mod[bsz:bsz + 1, i * d:(i + 1) * d], (bsz, d)) for i in range(2)]

    w_in_l = w_in[layer]
    n_rest = w_in_l.shape[1] - fw
    w_u = _cast_pad(w_in_l, 0, fw, d, fw, d, MXU_WIDTH)
    w_rest = _cast_pad(w_in_l, fw, n_rest, d, n_rest + (-n_rest % IN_PROJ_TILE), d, MXU_WIDTH)
    conv_p = (ssd_conv_w[layer], ssd_conv_b[layer], ssd_dt_bias[layer], ssd_a_log[layer])

    x2d = x.reshape(t, d)
    hc = _lnmod(ctx.reshape(bsz * ctx_len, d), norm_mix_w[layer], csh_m, csc_m, ctx_len, ctx_len)
    _, act_c, prep_c = _ssd_branch_inputs(hc, w_rest, *conv_p, ctx_len, fw, heads, _tile(bsz * ctx_len, 1024))
    h_zero = jnp.zeros((bsz, SSD_GROUPS, (heads // SSD_GROUPS) * SSD_HEAD_DIM, SSD_STATE), F32)
    _, hc_f = _ssd_scan(act_c, prep_c, h_zero, ctx_len, heads, False)
    _, hc_b = _ssd_scan(act_c, prep_c, h_zero, ctx_len, heads, True)

    tm = _tile(seq, 1024)
    tr = _tile(seq, 256)
    hx = _lnmod(x2d, norm_mix_w[layer], sh_m, sc_m, seq, tr)
    u = _mm(hx, w_u, tm, 512, F32, "in_proj_fnet")
    f_out = _fourier_mix(u, fnet_w[layer], bsz, seq)
    rest, act, prep = _ssd_branch_inputs(hx, w_rest, *conv_p, seq, fw, heads, tm)
    y_f, _ = _ssd_scan(act, prep, hc_f, seq, heads, False)
    y_b, _ = _ssd_scan(act, prep, hc_b, seq, heads, True)
    d_exp = jnp.repeat(ssd_d[layer], SSD_HEAD_DIM)
    s_out = _ssd_out(y_f, y_b, act, rest, d_exp, ssd_norm_w[layer], tr)
    x1 = _wout(f_out, s_out, _cast_pad(w_out[layer], 0, d, d, d, FFN_TILE, d), x2d, g_m, seq, tm, 512)

    hx2 = _lnmod(x1, norm_ffn_w[layer], sh_f, sc_f, seq, tr)
    f = ffn_w_gate.shape[2]
    fpad = -f % FFN_TILE
    w_gate = _cast_pad(ffn_w_gate[layer], 0, f, d, f + fpad, d, FFN_TILE)
    w_val = _cast_pad(ffn_w_val[layer], 0, f, d, f + fpad, d, FFN_TILE)
    w_down = _cast_pad(ffn_w_down[layer], 0, d, f + fpad, d, FFN_TILE, d)
    gate, val = _mm_pair(hx2, w_gate, w_val, tm, FFN_TILE, BF16, BF16, "ffn_up")
    conv_w9 = jnp.pad(ffn_conv_w[layer].reshape(9, f), ((0, 0), (0, fpad)))
    out = _ffn_down(gate, val, conv_w9, jnp.pad(ffn_conv_b[layer], (0, fpad)),
                    w_down, x1, g_f, norm_final_w, seq, _tile(seq, 512), FFN_TILE)
    return out.reshape(bsz, seq, d)
```

```python
import functools
import math

import numpy as np
import jax
import jax.numpy as jnp
from jax import lax
from jax.experimental import pallas as pl
from jax.experimental.pallas import tpu as pltpu

F32 = jnp.float32
BF16 = jnp.bfloat16
HIGHEST = lax.Precision.HIGHEST

GRID_W = 64
SSD_HEAD_DIM = 64
SSD_GROUPS = 8
SSD_STATE = 128
SSD_CHUNK = 128
EPS = 1e-6

LANES = 128
SUBLANES = 8
MXU_WIDTH = 256
IN_PROJ_TILE = 768
VMEM_LIMIT = 56 * 1024 * 1024
FFN_TILE = 512
FFN_DOWN_SLAB = 512


def _cparams(*sem):
    return pltpu.CompilerParams(dimension_semantics=sem, vmem_limit_bytes=VMEM_LIMIT)


def _silu(v):
    return v * jax.nn.sigmoid(v)


def _adaln_kernel(c_ref, w_ref, b_ref, o_ref):
    c = c_ref[...]
    s = _silu(c).astype(BF16)
    o_ref[...] = jnp.dot(s, w_ref[...].astype(BF16), preferred_element_type=F32) + b_ref[...]


def _adaln(cvecs, w, b):
    d, n = w.shape
    tn = 512
    return pl.pallas_call(
        _adaln_kernel,
        grid=(n // tn,),
        in_specs=[pl.BlockSpec((SUBLANES, d), lambda j: (0, 0)),
                  pl.BlockSpec((d, tn), lambda j: (0, j)),
                  pl.BlockSpec((1, tn), lambda j: (0, j))],
        out_specs=pl.BlockSpec((SUBLANES, tn), lambda j: (0, j)),
        out_shape=jax.ShapeDtypeStruct((SUBLANES, n), F32),
        compiler_params=_cparams("parallel"),
        name="adaln",
    )(cvecs, w, b.reshape(1, n))


def _lnmod_kernel(x_ref, w_ref, sh_ref, sc_ref, o_ref):
    x = x_ref[...]
    r = lax.rsqrt(jnp.mean(x * x, axis=-1, keepdims=True) + EPS)
    o_ref[...] = ((x * r) * w_ref[...] * (1.0 + sc_ref[0]) + sh_ref[0]).astype(o_ref.dtype)


def _lnmod(x2d, w, shift, scale, seq, tm):
    t, d = x2d.shape
    bsz = shift.shape[0]
    per = seq // tm
    return pl.pallas_call(
        _lnmod_kernel,
        grid=(t // tm,),
        in_specs=[pl.BlockSpec((tm, d), lambda i: (i, 0)),
                  pl.BlockSpec((1, d), lambda i: (0, 0)),
                  pl.BlockSpec((1, 1, d), lambda i: (i // per, 0, 0)),
                  pl.BlockSpec((1, 1, d), lambda i: (i // per, 0, 0))],
        out_specs=pl.BlockSpec((tm, d), lambda i: (i, 0)),
        out_shape=jax.ShapeDtypeStruct((t, d), BF16),
        compiler_params=_cparams("parallel"),
        name="lnmod",
    )(x2d, w.reshape(1, d), shift.reshape(bsz, 1, d), scale.reshape(bsz, 1, d))


def _cast_pad_kernel(w_ref, o_ref, *, rows, cols, tr, tc):
    r = lax.broadcasted_iota(jnp.int32, (tr, tc), 0) + pl.program_id(0) * tr
    c = lax.broadcasted_iota(jnp.int32, (tr, tc), 1) + pl.program_id(1) * tc
    o_ref[...] = jnp.where(jnp.logical_and(r < rows, c < cols), w_ref[...], 0.0).astype(o_ref.dtype)


def _cast_pad(w, row0, rows, out_rows, out_cols, tr, tc):
    cols = w.shape[1]
    assert row0 % tr == 0 and out_rows % tr == 0 and out_cols % tc == 0
    rb0 = row0 // tr
    return pl.pallas_call(
        functools.partial(_cast_pad_kernel, rows=rows, cols=cols, tr=tr, tc=tc),
        grid=(out_rows // tr, out_cols // tc),
        in_specs=[pl.BlockSpec((tr, tc), lambda i, j: (rb0 + i, j))],
        out_specs=pl.BlockSpec((tr, tc), lambda i, j: (i, j)),
        out_shape=jax.ShapeDtypeStruct((out_rows, out_cols), BF16),
        compiler_params=_cparams("parallel", "parallel"),
        name="weight_cast",
    )(w)


def _mm_nt_kernel(a_ref, wt_ref, o_ref):
    o_ref[...] = lax.dot_general(a_ref[...], wt_ref[...], (((1,), (1,)), ((), ())),
                                 preferred_element_type=F32).astype(o_ref.dtype)


def _mm_nt(a, wt, tm, tn, out_dtype, name):
    m, k = a.shape
    n = wt.shape[0]
    return pl.pallas_call(
        _mm_nt_kernel,
        grid=(m // tm, n // tn),
        in_specs=[pl.BlockSpec((tm, k), lambda i, j: (i, 0)),
                  pl.BlockSpec((tn, k), lambda i, j: (j, 0))],
        out_specs=pl.BlockSpec((tm, tn), lambda i, j: (i, j)),
        out_shape=jax.ShapeDtypeStruct((m, n), out_dtype),
        compiler_params=_cparams("parallel", "arbitrary"),
        name=name,
    )(a, wt)


def _mm_pair_kernel(a_ref, w1_ref, w2_ref, o1_ref, o2_ref):
    a = a_ref[...]
    o1_ref[0] = jnp.dot(a, w1_ref[...], preferred_element_type=F32).astype(o1_ref.dtype)
    o2_ref[0] = jnp.dot(a, w2_ref[...], preferred_element_type=F32).astype(o2_ref.dtype)


def _mm_pair(a, w1, w2, tm, tn, name):
    m, k = a.shape
    n = w1.shape[1]
    out = jax.ShapeDtypeStruct((n // tn, m, tn), BF16)
    return pl.pallas_call(
        _mm_pair_kernel,
        grid=(m // tm, n // tn),
        in_specs=[pl.BlockSpec((tm, k), lambda i, j: (i, 0)),
                  pl.BlockSpec((k, tn), lambda i, j: (0, j)),
                  pl.BlockSpec((k, tn), lambda i, j: (0, j))],
        out_specs=[pl.BlockSpec((1, tm, tn), lambda i, j: (j, i, 0)),
                   pl.BlockSpec((1, tm, tn), lambda i, j: (j, i, 0))],
        out_shape=[out, out],
        compiler_params=_cparams("parallel", "arbitrary"),
        name=name,
    )(a, w1, w2)


def _dft_tables(n):
    idx = np.arange(n)
    ang = 2.0 * np.pi * ((idx[:, None] * idx[None, :]) % n) / n
    return np.cos(ang), np.sin(ang)


def _fold_kernel(cs_ref, wf_ref, o_ref):
    o_ref[0] = jnp.dot(cs_ref[...], wf_ref[0], precision=HIGHEST,
                       preferred_element_type=F32).astype(o_ref.dtype)


def _fourier_fold(fnet_w, seq):
    g, c, _ = fnet_w.shape
    cc, sc = _dft_tables(c)
    cs = jnp.asarray(np.concatenate([cc, sc], axis=0) / math.sqrt(seq * c), F32)
    return pl.pallas_call(
        _fold_kernel,
        grid=(g,),
        in_specs=[pl.BlockSpec((2 * c, c), lambda i: (0, 0)),
                  pl.BlockSpec((1, c, c), lambda i: (i, 0, 0))],
        out_specs=pl.BlockSpec((1, 2 * c, c), lambda i: (i, 0, 0)),
        out_shape=jax.ShapeDtypeStruct((g, 2 * c, c), BF16),
        compiler_params=_cparams("parallel"),
        name="fnet_fold",
    )(cs, fnet_w)


def _dft1_kernel(f_ref, u_ref, zr_ref, zi_ref):
    n1 = zr_ref.shape[1]
    f = f_ref[...]
    for j in range(u_ref.shape[2]):
        z = jnp.dot(f, u_ref[0, :, j, :].astype(BF16), preferred_element_type=F32)
        zr_ref[0, :, j, :] = z[:n1]
        zi_ref[0, :, j, :] = z[n1:]


def _dft_stage1(u4):
    bsz, n1, n2, wdt = u4.shape
    c1, s1 = _dft_tables(n1)
    f1 = jnp.asarray(np.concatenate([c1, -s1], axis=0), BF16)
    spec = pl.BlockSpec((1, n1, SUBLANES, wdt), lambda b, j: (b, 0, j, 0))
    return pl.pallas_call(
        _dft1_kernel,
        grid=(bsz, n2 // SUBLANES),
        in_specs=[pl.BlockSpec((2 * n1, n1), lambda b, j: (0, 0)), spec],
        out_specs=[spec, spec],
        out_shape=[jax.ShapeDtypeStruct(u4.shape, F32)] * 2,
        compiler_params=_cparams("parallel", "parallel"),
        name="fnet_dft_stage1",
    )(f1, u4)


def _dft2_kernel(c2_ref, s2_ref, tc_ref, ts_ref, wcs_ref, zr_ref, zi_ref, o_ref, *, kb, groups, cg):
    c2 = c2_ref[...]
    s2 = s2_ref[...]
    for j in range(kb):
        tcj = tc_ref[j:j + 1, :]
        tsj = ts_ref[j:j + 1, :]
        gc = c2 * tcj - s2 * tsj
        gs = s2 * tcj + c2 * tsj
        gmat = jnp.concatenate([jnp.concatenate([gc, gs], axis=1),
                                jnp.concatenate([-gs, gc], axis=1)], axis=0).astype(BF16)
        z = jnp.concatenate([zr_ref[0, j], zi_ref[0, j]], axis=0).astype(BF16)
        p = jnp.dot(gmat, z, preferred_element_type=F32)
        n2 = c2.shape[0]
        pr = p[:n2].astype(BF16)
        pi = p[n2:].astype(BF16)
        for g in range(groups):
            sl = slice(g * cg, (g + 1) * cg)
            y = jnp.dot(jnp.concatenate([pr[:, sl], pi[:, sl]], axis=1), wcs_ref[g],
                        preferred_element_type=F32)
            o_ref[0, :, j, sl] = y.astype(o_ref.dtype)


def _dft_stage2(zr4, zi4, wcs, seq):
    bsz, n1, n2, wdt = zr4.shape
    groups, _, cg = wcs.shape
    c2, s2 = _dft_tables(n2)
    k1 = np.arange(n1)[:, None]
    l2 = np.arange(n2)[None, :]
    ang = 2.0 * np.pi * ((k1 * l2) % seq) / seq
    kb = SUBLANES
    zspec = pl.BlockSpec((1, kb, n2, wdt), lambda b, i: (b, i, 0, 0))
    full = lambda shape: pl.BlockSpec(shape, lambda b, i: (0,) * len(shape))
    return pl.pallas_call(
        functools.partial(_dft2_kernel, kb=kb, groups=groups, cg=cg),
        grid=(bsz, n1 // kb),
        in_specs=[full((n2, n2)), full((n2, n2)),
                  pl.BlockSpec((kb, n2), lambda b, i: (i, 0)),
                  pl.BlockSpec((kb, n2), lambda b, i: (i, 0)),
                  full(wcs.shape), zspec, zspec],
        out_specs=pl.BlockSpec((1, n2, kb, wdt), lambda b, i: (b, 0, i, 0)),
        out_shape=jax.ShapeDtypeStruct((bsz, n2, n1, wdt), BF16),
        compiler_params=_cparams("parallel", "parallel"),
        name="fnet_dft_stage2",
    )(jnp.asarray(c2, F32), jnp.asarray(s2, F32), jnp.asarray(np.cos(ang), F32),
      jnp.asarray(np.sin(ang), F32), wcs, zr4, zi4)


def _fourier_mix(u, fnet_w, bsz, seq):
    wdt = u.shape[1]
    n2 = LANES
    n1 = seq // n2
    wcs = _fourier_fold(fnet_w, seq)
    zr, zi = _dft_stage1(u.reshape(bsz, n1, n2, wdt))
    out = _dft_stage2(zr, zi, wcs, seq)
    return out.reshape(bsz * seq, wdt)


def _dwconv_silu_kernel(prev_ref, main_ref, next_ref, w_ref, b_ref, o_ref, scr, *, per, tl, taps):
    t = pl.program_id(0) % per
    halo = SUBLANES
    scr[0:halo, :] = jnp.where(t == 0, 0.0, prev_ref[...])
    scr[halo:halo + tl, :] = main_ref[...]
    scr[halo + tl:2 * halo + tl, :] = jnp.where(t == per - 1, 0.0, next_ref[...])
    acc = b_ref[...] + w_ref[0:1, :] * scr[pl.ds(halo - (taps - 1) // 2, tl), :]
    for k in range(1, taps):
        acc = acc + w_ref[k:k + 1, :] * scr[pl.ds(halo - (taps - 1) // 2 + k, tl), :]
    o_ref[...] = _silu(acc).astype(o_ref.dtype)


def _dwconv_silu(src, col0, width, conv_w, conv_b, seq, tl, tc):
    t = src.shape[0]
    taps = conv_w.shape[0]
    per = seq // tl
    cb0 = col0 // tc
    hb = tl // SUBLANES
    last = t // SUBLANES - 1
    return pl.pallas_call(
        functools.partial(_dwconv_silu_kernel, per=per, tl=tl, taps=taps),
        grid=(t // tl, width // tc),
        in_specs=[pl.BlockSpec((SUBLANES, tc), lambda i, j: (jnp.maximum(i * hb - 1, 0), cb0 + j)),
                  pl.BlockSpec((tl, tc), lambda i, j: (i, cb0 + j)),
                  pl.BlockSpec((SUBLANES, tc), lambda i, j: (jnp.minimum((i + 1) * hb, last), cb0 + j)),
                  pl.BlockSpec((taps, tc), lambda i, j: (0, j)),
                  pl.BlockSpec((1, tc), lambda i, j: (0, j))],
        out_specs=pl.BlockSpec((tl, tc), lambda i, j: (i, j)),
        out_shape=jax.ShapeDtypeStruct((t, width), BF16),
        scratch_shapes=[pltpu.VMEM((tl + 2 * SUBLANES, tc), F32)],
        compiler_params=_cparams("parallel", "parallel"),
        name="ssd_dwconv_silu",
    )(src, src, src, conv_w, conv_b.reshape(1, width))


def _ssd_prep_kernel(raw_ref, bias_ref, alog_ref, dtT_ref, acol_ref, aT_ref, w_ref, *, heads):
    c = SSD_CHUNK
    v = raw_ref[...] + bias_ref[...]
    dt = jnp.maximum(v, 0.0) + jnp.log1p(jnp.exp(-jnp.abs(v)))
    lane = lax.broadcasted_iota(jnp.int32, (c, LANES), 1)
    dt = jnp.where(lane < 2 * heads, dt, 0.0)
    dta = dt * (-jnp.exp(alog_ref[...]))
    li = lax.broadcasted_iota(jnp.int32, (c, c), 0)
    si = lax.broadcasted_iota(jnp.int32, (c, c), 1)
    lower = (si <= li).astype(F32)
    upper = (si >= li).astype(F32)
    pre = jnp.dot(lower, dta, precision=HIGHEST, preferred_element_type=F32)
    suf = jnp.dot(upper, dta, precision=HIGHEST, preferred_element_type=F32)
    acum = jnp.where(lane < heads, pre, suf)
    lane_row = lax.broadcasted_iota(jnp.int32, (1, LANES), 1)
    aend = jnp.where(lane_row < heads, acum[c - 1:c, :], acum[0:1, :])
    dtT_ref[...] = dt.T
    acol_ref[...] = acum
    aT_ref[...] = acum.T
    w_ref[...] = (dt * jnp.exp(aend - acum)).T


def _ssd_prep(src, colblk, dt_bias, a_log, heads):
    t = src.shape[0]
    c = SSD_CHUNK
    pad = LANES - 2 * heads
    bias = jnp.pad(dt_bias.reshape(1, 2 * heads), ((0, 0), (0, pad)))
    alog = jnp.pad(a_log.reshape(1, 2 * heads), ((0, 0), (0, pad)))
    spec = pl.BlockSpec((c, LANES), lambda i: (i, 0))
    vec = pl.BlockSpec((1, LANES), lambda i: (0, 0))
    return pl.pallas_call(
        functools.partial(_ssd_prep_kernel, heads=heads),
        grid=(t // c,),
        in_specs=[pl.BlockSpec((c, LANES), lambda i: (i, colblk)), vec, vec],
        out_specs=[spec] * 4,
        out_shape=[jax.ShapeDtypeStruct((t, LANES), F32)] * 4,
        compiler_params=_cparams("parallel"),
        name="ssd_prep",
    )(src, bias, alog)


def _ssd_scan_kernel(x_ref, b_ref, c_ref, dtT_ref, acol_ref, aT_ref, w_ref, h0_ref, y_ref, hout_ref, h_scr,
                     *, reverse, col0, groups, hpg, nsteps):
    c = SSD_CHUNK
    n = SSD_STATE
    p = SSD_HEAD_DIM
    bsz = x_ref.shape[0]
    s = pl.program_id(0)

    @pl.when(s == 0)
    def _():
        h_scr[...] = h0_ref[...]

    li = lax.broadcasted_iota(jnp.int32, (c, c), 0)
    si = lax.broadcasted_iota(jnp.int32, (c, c), 1)
    mask = (li <= si) if reverse else (li >= si)
    lo_half = lax.broadcasted_iota(jnp.int32, (c, LANES), 1) < p
    pairs = hpg // 2
    nt = (((1,), (1,)), ((), ()))

    def stream(b):
        acol = acol_ref[b]
        a_t = aT_ref[b]
        dt_t = dtT_ref[b]
        w_t = w_ref[b]
        eend = jnp.exp(acol[0:1, :] if reverse else acol[c - 1:c, :])
        shared = {}

        def group_piece(g):
            bg = b_ref[b, :, g * n:(g + 1) * n]
            cg = c_ref[b, :, g * n:(g + 1) * n]
            h = h_scr[b, g]
            shared[g] = (bg,
                         lax.dot_general(cg, bg, nt, preferred_element_type=F32),
                         h,
                         lax.dot_general(cg, h.astype(BF16), nt, preferred_element_type=F32))

        def pair_piece(g, pr):
            bg, cb, h, yoff = shared[g]
            lanes = slice((g * pairs + pr) * LANES, (g * pairs + pr + 1) * LANES)
            loc = slice(pr * LANES, (pr + 1) * LANES)
            xpb = x_ref[b, :, lanes]
            k0 = col0 + g * hpg + 2 * pr
            ys, abc = [], []
            for q in range(2):
                k = k0 + q
                abc.append(jnp.broadcast_to(acol[:, k:k + 1], (c, c)))
                dec = jnp.exp(jnp.where(mask, abc[q] - a_t[k:k + 1, :], -jnp.inf))
                m = (cb * dec * dt_t[k:k + 1, :]).astype(BF16)
                ys.append(jnp.dot(m, xpb, preferred_element_type=F32))
            e_pair = jnp.exp(jnp.where(lo_half, abc[0], abc[1]))
            y = jnp.where(lo_half, ys[0], ys[1]) + yoff[:, loc] * e_pair
            y_ref[b, :, lanes] = y.astype(y_ref.dtype)
            per_head = lambda v0, v1, width: jnp.concatenate(
                [jnp.broadcast_to(v0, (p, width)), jnp.broadcast_to(v1, (p, width))], axis=0)
            xw_t = (xpb.astype(F32).T * per_head(w_t[k0:k0 + 1, :], w_t[k0 + 1:k0 + 2, :], c)).astype(BF16)
            upd = jnp.dot(xw_t, bg, preferred_element_type=F32)
            h_scr[b, g, loc, :] = h[loc] * per_head(eend[:, k0:k0 + 1], eend[:, k0 + 1:k0 + 2], n) + upd

        pieces = []
        for g in range(groups):
            pieces.append(functools.partial(group_piece, g))
            pieces.extend(functools.partial(pair_piece, g, pr) for pr in range(pairs))
        return pieces

    for step in zip(*[stream(b) for b in range(bsz)]):
        for piece in step:
            piece()

    @pl.when(s == nsteps - 1)
    def _():
        hout_ref[...] = h_scr[...]


def _ssd_scan(act, prep, h0, seq, heads, reverse):
    t = act.shape[0]
    bsz = t // seq
    c, n, groups = SSD_CHUNK, SSD_STATE, SSD_GROUPS
    hpg = heads // groups
    width = heads * SSD_HEAD_DIM
    gn = groups * n
    nsteps = seq // c
    act3 = act.reshape(bsz, seq, act.shape[1])
    prep3 = [p.reshape(bsz, seq, LANES) for p in prep]

    def chunk(s):
        return (nsteps - 1 - s) if reverse else s

    small = pl.BlockSpec((bsz, c, LANES), lambda s: (0, chunk(s), 0))
    hspec = pl.BlockSpec(h0.shape, lambda s: (0, 0, 0, 0))
    y, h = pl.pallas_call(
        functools.partial(_ssd_scan_kernel, reverse=reverse, col0=heads if reverse else 0,
                          groups=groups, hpg=hpg, nsteps=nsteps),
        grid=(nsteps,),
        in_specs=[pl.BlockSpec((bsz, c, width), lambda s: (0, chunk(s), 0)),
                  pl.BlockSpec((bsz, c, gn), lambda s: (0, chunk(s), width // gn)),
                  pl.BlockSpec((bsz, c, gn), lambda s: (0, chunk(s), width // gn + 1)),
                  small, small, small, small, hspec],
        out_specs=[pl.BlockSpec((bsz, c, width), lambda s: (0, chunk(s), 0)), hspec],
        out_shape=[jax.ShapeDtypeStruct((bsz, seq, width), BF16),
                   jax.ShapeDtypeStruct(h0.shape, F32)],
        scratch_shapes=[pltpu.VMEM(h0.shape, F32)],
        compiler_params=_cparams("arbitrary"),
        name="ssd_scan_bwd" if reverse else "ssd_scan_fwd",
    )(act3, act3, act3, *prep3, h0)
    return y.reshape(t, width), h


def _ssd_out_kernel(yf_ref, yb_ref, x_ref, z_ref, d_ref, nw_ref, o_ref, *, groups, gw):
    y = yf_ref[...].astype(F32) + yb_ref[...].astype(F32) + d_ref[...] * x_ref[...].astype(F32)
    g = y * _silu(z_ref[...])
    for i in range(groups):
        sl = slice(i * gw, (i + 1) * gw)
        gi = g[:, sl]
        r = lax.rsqrt(jnp.mean(gi * gi, axis=-1, keepdims=True) + EPS)
        o_ref[:, sl] = (gi * r * nw_ref[:, sl]).astype(o_ref.dtype)


def _ssd_out(yf, yb, act, rest, d_exp, norm_w, tm):
    t, width = yf.shape
    gw = width // SSD_GROUPS
    row = pl.BlockSpec((tm, width), lambda i: (i, 0))
    vec = pl.BlockSpec((1, width), lambda i: (0, 0))
    return pl.pallas_call(
        functools.partial(_ssd_out_kernel, groups=SSD_GROUPS, gw=gw),
        grid=(t // tm,),
        in_specs=[row, row, row, row, vec, vec],
        out_specs=row,
        out_shape=jax.ShapeDtypeStruct((t, width), BF16),
        compiler_params=_cparams("parallel"),
        name="ssd_out",
    )(yf, yb, act, rest, d_exp.reshape(1, width), norm_w.reshape(1, width))


def _wout_kernel(f_ref, s_ref, w_ref, x_ref, g_ref, o_ref):
    k1 = f_ref.shape[1]
    acc = jnp.dot(f_ref[...], w_ref[:k1, :], preferred_element_type=F32)
    acc = acc + jnp.dot(s_ref[...], w_ref[k1:, :], preferred_element_type=F32)
    o_ref[...] = x_ref[...] + g_ref[0] * acc


def _wout(f_out, s_out, w, x2d, gate, seq, tm, tn):
    t, d = x2d.shape
    bsz = gate.shape[0]
    k1, k2 = f_out.shape[1], s_out.shape[1]
    per = seq // tm
    return pl.pallas_call(
        _wout_kernel,
        grid=(t // tm, d // tn),
        in_specs=[pl.BlockSpec((tm, k1), lambda i, j: (i, 0)),
                  pl.BlockSpec((tm, k2), lambda i, j: (i, 0)),
                  pl.BlockSpec((k1 + k2, tn), lambda i, j: (0, j)),
                  pl.BlockSpec((tm, tn), lambda i, j: (i, j)),
                  pl.BlockSpec((1, 1, tn), lambda i, j: (i // per, 0, j))],
        out_specs=pl.BlockSpec((tm, tn), lambda i, j: (i, j)),
        out_shape=jax.ShapeDtypeStruct((t, d), F32),
        compiler_params=_cparams("parallel", "arbitrary"),
        name="out_proj",
    )(f_out, s_out, w, x2d, gate.reshape(bsz, 1, d))


def _ffn_down_kernel(up_ref, main_ref, dn_ref, val_ref, cw_ref, cb_ref, w_ref, x_ref, g_ref, nw_ref, o_ref,
                     a0_scr, a1_scr, *, per, nk, nsteps):
    q = pl.program_id(0)
    qc = jnp.minimum(q, nsteps - 1)
    t = (qc // nk) % per
    kc = qc % nk
    km = jnp.maximum(q - 1, 0) % nk
    gw = GRID_W
    tm, tk = a0_scr.shape

    @pl.when(q == 0)
    def _():
        a1_scr[...] = jnp.zeros((tm, tk), BF16)

    @pl.when(km == 0)
    def _():
        o_ref[...] = jnp.zeros_like(o_ref)

    def conv_and_matmul(a_new, a_prev):
        col = lax.broadcasted_iota(jnp.int32, (gw, LANES), 0)
        not_first = col != 0
        not_last = col != gw - 1
        nrows = tm // gw

        def conv_piece(c, r):
            ls = slice(c * LANES, (c + 1) * LANES)
            cw = cw_ref[kc, :, ls]
            rows = slice(r * gw, (r + 1) * gw)
            if r == 0:
                above = jnp.where(t == 0, 0.0, up_ref[0, :, ls].astype(F32))
            else:
                above = main_ref[0, (r - 1) * gw:r * gw, ls].astype(F32)
            mid = main_ref[0, rows, ls].astype(F32)
            if r == nrows - 1:
                below = jnp.where(t == per - 1, 0.0, dn_ref[0, :, ls].astype(F32))
            else:
                below = main_ref[0, (r + 1) * gw:(r + 2) * gw, ls].astype(F32)
            p = [cw[dx:dx + 1] * above + cw[3 + dx:4 + dx] * mid + cw[6 + dx:7 + dx] * below
                 for dx in range(3)]
            left = jnp.where(not_first, pltpu.roll(p[0], 1, 0), 0.0)
            right = jnp.where(not_last, pltpu.roll(p[2], gw - 1, 0), 0.0)
            acc = p[1] + cb_ref[kc, :, ls] + left + right
            a_new[rows, ls] = (_silu(acc) * val_ref[0, rows, ls].astype(F32)).astype(BF16)

        def matmul_piece(n):
            ns = slice(n * FFN_DOWN_SLAB, (n + 1) * FFN_DOWN_SLAB)
            o_ref[:, ns] += jnp.dot(a_prev[...], w_ref[:, ns], preferred_element_type=F32)

        pieces = [(c, r) for c in range(tk // LANES) for r in range(nrows)]
        nslabs = o_ref.shape[1] // FFN_DOWN_SLAB
        per_slab = -(-len(pieces) // nslabs)
        for n in range(nslabs):
            matmul_piece(n)
            for c, r in pieces[n * per_slab:(n + 1) * per_slab]:
                conv_piece(c, r)

    @pl.when(q % 2 == 0)
    def _():
        conv_and_matmul(a0_scr, a1_scr)

    @pl.when(q % 2 == 1)
    def _():
        conv_and_matmul(a1_scr, a0_scr)

    @pl.when(jnp.logical_and(km == nk - 1, q > 0))
    def _():
        def body(r, carry):
            rows = pl.ds(pl.multiple_of(r * SUBLANES, SUBLANES), SUBLANES)
            v = x_ref[rows, :] + g_ref[0] * o_ref[rows, :]
            s = lax.rsqrt(jnp.mean(v * v, axis=-1, keepdims=True) + EPS)
            o_ref[rows, :] = v * s * nw_ref[...]
            return carry
        lax.fori_loop(0, tm // SUBLANES, body, 0)


def _ffn_down(gate, val, conv_w9, conv_b, w, x2d, g, norm_w, seq, tm):
    nk, t, tk = gate.shape
    d = x2d.shape[1]
    bsz = g.shape[0]
    per = seq // tm
    hb = tm // GRID_W
    last = t // GRID_W - 1
    nsteps = (t // tm) * nk
    conv_w3 = conv_w9.reshape(9, nk, tk).swapaxes(0, 1)
    conv_b3 = conv_b.reshape(nk, 1, tk)

    def conv_tile(q):
        qc = jnp.minimum(q, nsteps - 1)
        return qc // nk, qc % nk

    def mm_tile(q):
        qm = jnp.maximum(q - 1, 0)
        return qm // nk, qm % nk

    def main_map(q):
        i, k = conv_tile(q)
        return k, i, 0

    def up_map(q):
        i, k = conv_tile(q)
        return k, jnp.maximum(i * hb - 1, 0), 0

    def dn_map(q):
        i, k = conv_tile(q)
        return k, jnp.minimum((i + 1) * hb, last), 0

    return pl.pallas_call(
        functools.partial(_ffn_down_kernel, per=per, nk=nk, nsteps=nsteps),
        grid=(nsteps + 1,),
        in_specs=[pl.BlockSpec((1, GRID_W, tk), up_map),
                  pl.BlockSpec((1, tm, tk), main_map),
                  pl.BlockSpec((1, GRID_W, tk), dn_map),
                  pl.BlockSpec((1, tm, tk), main_map),
                  pl.BlockSpec((nk, 9, tk), lambda q: (0, 0, 0)),
                  pl.BlockSpec((nk, 1, tk), lambda q: (0, 0, 0)),
                  pl.BlockSpec((tk, d), lambda q: (mm_tile(q)[1], 0)),
                  pl.BlockSpec((tm, d), lambda q: (mm_tile(q)[0], 0)),
                  pl.BlockSpec((1, 1, d), lambda q: (mm_tile(q)[0] // per, 0, 0)),
                  pl.BlockSpec((1, d), lambda q: (0, 0))],
        out_specs=pl.BlockSpec((tm, d), lambda q: (mm_tile(q)[0], 0)),
        out_shape=jax.ShapeDtypeStruct((t, d), F32),
        scratch_shapes=[pltpu.VMEM((tm, tk), BF16), pltpu.VMEM((tm, tk), BF16)],
        compiler_params=_cparams("arbitrary"),
        name="ffn_conv_down",
    )(gate, gate, gate, val, conv_w3, conv_b3, w, x2d, g.reshape(bsz, 1, d), norm_w.reshape(1, d))


def _tile(n, pref):
    t = pref
    while n % t:
        t //= 2
    return t


def _ssd_branch_inputs(h, w_rest, conv_w, conv_b, dt_bias, a_log, seq, fw, heads, tm):
    width = heads * SSD_HEAD_DIM
    conv_dim = conv_w.shape[1]
    rest = _mm_nt(h, w_rest, tm, IN_PROJ_TILE, F32, "in_proj_ssd")
    act = _dwconv_silu(rest, width, conv_dim, conv_w, conv_b, seq, _tile(seq, 512), 512)
    prep = _ssd_prep(rest, (width + conv_dim) // LANES, dt_bias, a_log, heads)
    return rest, act, prep


def kernel(x, c, ctx, c_ctx, ada_w, ada_b, norm_mix_w, w_in, fnet_w, ssd_conv_w, ssd_conv_b, ssd_dt_bias,
           ssd_a_log, ssd_d, ssd_norm_w, w_out, norm_ffn_w, ffn_w_gate, ffn_w_val, ffn_conv_w, ffn_conv_b,
           ffn_w_down, norm_final_w):
    bsz, seq, d = x.shape
    ctx_len = ctx.shape[1]
    depth = ada_w.shape[0]
    fw = fnet_w.shape[1] * fnet_w.shape[2]
    width = d - fw
    heads = width // SSD_HEAD_DIM
    conv_dim = ssd_conv_w.shape[2]
    assert depth == 1, "only the single-layer block is implemented"
    assert 2 * heads <= LANES and heads % SSD_GROUPS == 0 and (heads // SSD_GROUPS) % 2 == 0
    t = bsz * seq
    layer = 0

    cvecs = jnp.zeros((SUBLANES, d), F32).at[:bsz].set(c).at[bsz].set(c_ctx)
    mod = _adaln(cvecs, ada_w[layer], ada_b[layer])
    sh_m, sc_m, g_m, sh_f, sc_f, g_f = [mod[:bsz, i * d:(i + 1) * d] for i in range(6)]
    csh_m, csc_m = [jnp.broadcast_to(mod[bsz:bsz + 1, i * d:(i + 1) * d], (bsz, d)) for i in range(2)]

    w_in_t = jnp.swapaxes(w_in[layer], 0, 1)
    n_rest = w_in_t.shape[0] - fw
    w_u = _cast_pad(w_in_t, 0, fw, fw, d, MXU_WIDTH, d)
    w_rest = _cast_pad(w_in_t, fw, n_rest, n_rest + (-n_rest % IN_PROJ_TILE), d, MXU_WIDTH, d)
    conv_p = (ssd_conv_w[layer], ssd_conv_b[layer], ssd_dt_bias[layer], ssd_a_log[layer])

    x2d = x.reshape(t, d)
    hc = _lnmod(ctx.reshape(bsz * ctx_len, d), norm_mix_w[layer], csh_m, csc_m, ctx_len, ctx_len)
    _, act_c, prep_c = _ssd_branch_inputs(hc, w_rest, *conv_p, ctx_len, fw, heads, _tile(bsz * ctx_len, 1024))
    h_zero = jnp.zeros((bsz, SSD_GROUPS, (heads // SSD_GROUPS) * SSD_HEAD_DIM, SSD_STATE), F32)
    _, hc_f = _ssd_scan(act_c, prep_c, h_zero, ctx_len, heads, False)
    _, hc_b = _ssd_scan(act_c, prep_c, h_zero, ctx_len, heads, True)

    tm = _tile(seq, 1024)
    tr = _tile(seq, 256)
    hx = _lnmod(x2d, norm_mix_w[layer], sh_m, sc_m, seq, tr)
    u = _mm_nt(hx, w_u, tm, 512, F32, "in_proj_fnet")
    f_out = _fourier_mix(u, fnet_w[layer], bsz, seq)
    rest, act, prep = _ssd_branch_inputs(hx, w_rest, *conv_p, seq, fw, heads, tm)
    y_f, _ = _ssd_scan(act, prep, hc_f, seq, heads, False)
    y_b, _ = _ssd_scan(act, prep, hc_b, seq, heads, True)
    d_exp = jnp.repeat(ssd_d[layer], SSD_HEAD_DIM)
    s_out = _ssd_out(y_f, y_b, act, rest, d_exp, ssd_norm_w[layer], tr)
    x1 = _wout(f_out, s_out, _cast_pad(w_out[layer], 0, d, d, d, FFN_TILE, d), x2d, g_m, seq, tm, 1024)

    hx2 = _lnmod(x1, norm_ffn_w[layer], sh_f, sc_f, seq, tr)
    f = ffn_w_gate.shape[2]
    fpad = -f % FFN_TILE
    w_gate = _cast_pad(ffn_w_gate[layer], 0, d, d, f + fpad, d, FFN_TILE)
    w_val = _cast_pad(ffn_w_val[layer], 0, d, d, f + fpad, d, FFN_TILE)
    w_down = _cast_pad(ffn_w_down[layer], 0, f, f + fpad, d, FFN_TILE, d)
    gate, val = _mm_pair(hx2, w_gate, w_val, tm, FFN_TILE, "ffn_up")
    conv_w9 = jnp.pad(ffn_conv_w[layer].reshape(9, f), ((0, 0), (0, fpad)))
    out = _ffn_down(gate, val, conv_w9, jnp.pad(ffn_conv_b[layer], (0, fpad)),
                    w_down, x1, g_f, norm_final_w, seq, _tile(seq, 512))
    return out.reshape(bsz, seq, d)
```

```python
import functools
import math

import numpy as np
import jax
import jax.numpy as jnp
from jax import lax
from jax.experimental import pallas as pl
from jax.experimental.pallas import tpu as pltpu

F32 = jnp.float32
BF16 = jnp.bfloat16
HIGHEST = lax.Precision.HIGHEST

GRID_W = 64
SSD_HEAD_DIM = 64
SSD_GROUPS = 8
SSD_STATE = 128
SSD_CHUNK = 128
EPS = 1e-6

LANES = 128
SUBLANES = 8
MXU_WIDTH = 256
IN_PROJ_TILE = 768
VMEM_LIMIT = 56 * 1024 * 1024
FFN_TILE = 512
FFN_DOWN_SLAB = 512


def _cparams(*sem):
    return pltpu.CompilerParams(dimension_semantics=sem, vmem_limit_bytes=VMEM_LIMIT)


def _silu(v):
    return v * jax.nn.sigmoid(v)


def _adaln_kernel(c_ref, w_ref, b_ref, o_ref):
    c = c_ref[...]
    s = _silu(c).astype(BF16)
    o_ref[...] = jnp.dot(s, w_ref[...].astype(BF16), preferred_element_type=F32) + b_ref[...]


def _adaln(cvecs, w, b):
    d, n = w.shape
    tn = 512
    return pl.pallas_call(
        _adaln_kernel,
        grid=(n // tn,),
        in_specs=[pl.BlockSpec((SUBLANES, d), lambda j: (0, 0)),
                  pl.BlockSpec((d, tn), lambda j: (0, j)),
                  pl.BlockSpec((1, tn), lambda j: (0, j))],
        out_specs=pl.BlockSpec((SUBLANES, tn), lambda j: (0, j)),
        out_shape=jax.ShapeDtypeStruct((SUBLANES, n), F32),
        compiler_params=_cparams("parallel"),
        name="adaln",
    )(cvecs, w, b.reshape(1, n))


def _lnmod_kernel(x_ref, w_ref, sh_ref, sc_ref, o_ref):
    x = x_ref[...]
    r = lax.rsqrt(jnp.mean(x * x, axis=-1, keepdims=True) + EPS)
    o_ref[...] = ((x * r) * w_ref[...] * (1.0 + sc_ref[0]) + sh_ref[0]).astype(o_ref.dtype)


def _lnmod(x2d, w, shift, scale, seq, tm):
    t, d = x2d.shape
    bsz = shift.shape[0]
    per = seq // tm
    return pl.pallas_call(
        _lnmod_kernel,
        grid=(t // tm,),
        in_specs=[pl.BlockSpec((tm, d), lambda i: (i, 0)),
                  pl.BlockSpec((1, d), lambda i: (0, 0)),
                  pl.BlockSpec((1, 1, d), lambda i: (i // per, 0, 0)),
                  pl.BlockSpec((1, 1, d), lambda i: (i // per, 0, 0))],
        out_specs=pl.BlockSpec((tm, d), lambda i: (i, 0)),
        out_shape=jax.ShapeDtypeStruct((t, d), BF16),
        compiler_params=_cparams("parallel"),
        name="lnmod",
    )(x2d, w.reshape(1, d), shift.reshape(bsz, 1, d), scale.reshape(bsz, 1, d))


def _cast_pad_kernel(w_ref, o_ref, *, rows, cols, tr, tc):
    r = lax.broadcasted_iota(jnp.int32, (tr, tc), 0) + pl.program_id(0) * tr
    c = lax.broadcasted_iota(jnp.int32, (tr, tc), 1) + pl.program_id(1) * tc
    o_ref[...] = jnp.where(jnp.logical_and(r < rows, c < cols), w_ref[...], 0.0).astype(o_ref.dtype)


def _cast_pad(w, row0, rows, out_rows, out_cols, tr, tc):
    cols = w.shape[1]
    assert row0 % tr == 0 and out_rows % tr == 0 and out_cols % tc == 0
    rb0 = row0 // tr
    return pl.pallas_call(
        functools.partial(_cast_pad_kernel, rows=rows, cols=cols, tr=tr, tc=tc),
        grid=(out_rows // tr, out_cols // tc),
        in_specs=[pl.BlockSpec((tr, tc), lambda i, j: (rb0 + i, j))],
        out_specs=pl.BlockSpec((tr, tc), lambda i, j: (i, j)),
        out_shape=jax.ShapeDtypeStruct((out_rows, out_cols), BF16),
        compiler_params=_cparams("parallel", "parallel"),
        name="weight_cast",
    )(w)


def _mm_nt_kernel(a_ref, wt_ref, o_ref):
    o_ref[...] = lax.dot_general(a_ref[...], wt_ref[...], (((1,), (1,)), ((), ())),
                                 preferred_element_type=F32).astype(o_ref.dtype)


def _mm_nt(a, wt, tm, tn, out_dtype, name):
    m, k = a.shape
    n = wt.shape[0]
    return pl.pallas_call(
        _mm_nt_kernel,
        grid=(m // tm, n // tn),
        in_specs=[pl.BlockSpec((tm, k), lambda i, j: (i, 0)),
                  pl.BlockSpec((tn, k), lambda i, j: (j, 0))],
        out_specs=pl.BlockSpec((tm, tn), lambda i, j: (i, j)),
        out_shape=jax.ShapeDtypeStruct((m, n), out_dtype),
        compiler_params=_cparams("parallel", "arbitrary"),
        name=name,
    )(a, wt)


def _mm_pair_kernel(a_ref, w1_ref, w2_ref, o1_ref, o2_ref):
    a = a_ref[...]
    o1_ref[0] = jnp.dot(a, w1_ref[...], preferred_element_type=F32).astype(o1_ref.dtype)
    o2_ref[0] = jnp.dot(a, w2_ref[...], preferred_element_type=F32).astype(o2_ref.dtype)


def _mm_pair(a, w1, w2, tm, tn, name):
    m, k = a.shape
    n = w1.shape[1]
    out = jax.ShapeDtypeStruct((n // tn, m, tn), BF16)
    return pl.pallas_call(
        _mm_pair_kernel,
        grid=(m // tm, n // tn),
        in_specs=[pl.BlockSpec((tm, k), lambda i, j: (i, 0)),
                  pl.BlockSpec((k, tn), lambda i, j: (0, j)),
                  pl.BlockSpec((k, tn), lambda i, j: (0, j))],
        out_specs=[pl.BlockSpec((1, tm, tn), lambda i, j: (j, i, 0)),
                   pl.BlockSpec((1, tm, tn), lambda i, j: (j, i, 0))],
        out_shape=[out, out],
        compiler_params=_cparams("parallel", "arbitrary"),
        name=name,
    )(a, w1, w2)


def _dft_tables(n):
    idx = np.arange(n)
    ang = 2.0 * np.pi * ((idx[:, None] * idx[None, :]) % n) / n
    return np.cos(ang), np.sin(ang)


def _fold_kernel(cs_ref, wf_ref, o_ref):
    o_ref[0] = jnp.dot(cs_ref[...], wf_ref[0], precision=HIGHEST,
                       preferred_element_type=F32).astype(o_ref.dtype)


def _fourier_fold(fnet_w, seq):
    g, c, _ = fnet_w.shape
    cc, sc = _dft_tables(c)
    cs = jnp.asarray(np.concatenate([cc, sc], axis=0) / math.sqrt(seq * c), F32)
    return pl.pallas_call(
        _fold_kernel,
        grid=(g,),
        in_specs=[pl.BlockSpec((2 * c, c), lambda i: (0, 0)),
                  pl.BlockSpec((1, c, c), lambda i: (i, 0, 0))],
        out_specs=pl.BlockSpec((1, 2 * c, c), lambda i: (i, 0, 0)),
        out_shape=jax.ShapeDtypeStruct((g, 2 * c, c), BF16),
        compiler_params=_cparams("parallel"),
        name="fnet_fold",
    )(cs, fnet_w)


def _dft1_kernel(f_ref, u_ref, zr_ref, zi_ref):
    n1 = zr_ref.shape[1]
    f = f_ref[...]
    for j in range(u_ref.shape[2]):
        z = jnp.dot(f, u_ref[0, :, j, :].astype(BF16), preferred_element_type=F32)
        zr_ref[0, :, j, :] = z[:n1]
        zi_ref[0, :, j, :] = z[n1:]


def _dft_stage1(u4):
    bsz, n1, n2, wdt = u4.shape
    c1, s1 = _dft_tables(n1)
    f1 = jnp.asarray(np.concatenate([c1, -s1], axis=0), BF16)
    spec = pl.BlockSpec((1, n1, SUBLANES, wdt), lambda b, j: (b, 0, j, 0))
    return pl.pallas_call(
        _dft1_kernel,
        grid=(bsz, n2 // SUBLANES),
        in_specs=[pl.BlockSpec((2 * n1, n1), lambda b, j: (0, 0)), spec],
        out_specs=[spec, spec],
        out_shape=[jax.ShapeDtypeStruct(u4.shape, F32)] * 2,
        compiler_params=_cparams("parallel", "parallel"),
        name="fnet_dft_stage1",
    )(f1, u4)


def _dft2_kernel(c2_ref, s2_ref, tc_ref, ts_ref, wcs_ref, zr_ref, zi_ref, o_ref, *, kb, groups, cg):
    c2 = c2_ref[...]
    s2 = s2_ref[...]
    for j in range(kb):
        tcj = tc_ref[j:j + 1, :]
        tsj = ts_ref[j:j + 1, :]
        gc = c2 * tcj - s2 * tsj
        gs = s2 * tcj + c2 * tsj
        gmat = jnp.concatenate([jnp.concatenate([gc, gs], axis=1),
                                jnp.concatenate([-gs, gc], axis=1)], axis=0).astype(BF16)
        z = jnp.concatenate([zr_ref[0, j], zi_ref[0, j]], axis=0).astype(BF16)
        p = jnp.dot(gmat, z, preferred_element_type=F32)
        n2 = c2.shape[0]
        pr = p[:n2].astype(BF16)
        pi = p[n2:].astype(BF16)
        for g in range(groups):
            sl = slice(g * cg, (g + 1) * cg)
            y = jnp.dot(jnp.concatenate([pr[:, sl], pi[:, sl]], axis=1), wcs_ref[g],
                        preferred_element_type=F32)
            o_ref[0, :, j, sl] = y.astype(o_ref.dtype)


def _dft_stage2(zr4, zi4, wcs, seq):
    bsz, n1, n2, wdt = zr4.shape
    groups, _, cg = wcs.shape
    c2, s2 = _dft_tables(n2)
    k1 = np.arange(n1)[:, None]
    l2 = np.arange(n2)[None, :]
    ang = 2.0 * np.pi * ((k1 * l2) % seq) / seq
    kb = SUBLANES
    zspec = pl.BlockSpec((1, kb, n2, wdt), lambda b, i: (b, i, 0, 0))
    full = lambda shape: pl.BlockSpec(shape, lambda b, i: (0,) * len(shape))
    return pl.pallas_call(
        functools.partial(_dft2_kernel, kb=kb, groups=groups, cg=cg),
        grid=(bsz, n1 // kb),
        in_specs=[full((n2, n2)), full((n2, n2)),
                  pl.BlockSpec((kb, n2), lambda b, i: (i, 0)),
                  pl.BlockSpec((kb, n2), lambda b, i: (i, 0)),
                  full(wcs.shape), zspec, zspec],
        out_specs=pl.BlockSpec((1, n2, kb, wdt), lambda b, i: (b, 0, i, 0)),
        out_shape=jax.ShapeDtypeStruct((bsz, n2, n1, wdt), BF16),
        compiler_params=_cparams("parallel", "parallel"),
        name="fnet_dft_stage2",
    )(jnp.asarray(c2, F32), jnp.asarray(s2, F32), jnp.asarray(np.cos(ang), F32),
      jnp.asarray(np.sin(ang), F32), wcs, zr4, zi4)


def _fourier_mix(u, fnet_w, bsz, seq):
    wdt = u.shape[1]
    n2 = LANES
    n1 = seq // n2
    wcs = _fourier_fold(fnet_w, seq)
    zr, zi = _dft_stage1(u.reshape(bsz, n1, n2, wdt))
    out = _dft_stage2(zr, zi, wcs, seq)
    return out.reshape(bsz * seq, wdt)


def _dwconv_silu_kernel(prev_ref, main_ref, next_ref, w_ref, b_ref, o_ref, scr, *, per, tl, taps):
    t = pl.program_id(0) % per
    halo = SUBLANES
    scr[0:halo, :] = jnp.where(t == 0, 0.0, prev_ref[...])
    scr[halo:halo + tl, :] = main_ref[...]
    scr[halo + tl:2 * halo + tl, :] = jnp.where(t == per - 1, 0.0, next_ref[...])
    acc = b_ref[...] + w_ref[0:1, :] * scr[pl.ds(halo - (taps - 1) // 2, tl), :]
    for k in range(1, taps):
        acc = acc + w_ref[k:k + 1, :] * scr[pl.ds(halo - (taps - 1) // 2 + k, tl), :]
    o_ref[...] = _silu(acc).astype(o_ref.dtype)


def _dwconv_silu(src, col0, width, conv_w, conv_b, seq, tl, tc):
    t = src.shape[0]
    taps = conv_w.shape[0]
    per = seq // tl
    cb0 = col0 // tc
    hb = tl // SUBLANES
    last = t // SUBLANES - 1
    return pl.pallas_call(
        functools.partial(_dwconv_silu_kernel, per=per, tl=tl, taps=taps),
        grid=(t // tl, width // tc),
        in_specs=[pl.BlockSpec((SUBLANES, tc), lambda i, j: (jnp.maximum(i * hb - 1, 0), cb0 + j)),
                  pl.BlockSpec((tl, tc), lambda i, j: (i, cb0 + j)),
                  pl.BlockSpec((SUBLANES, tc), lambda i, j: (jnp.minimum((i + 1) * hb, last), cb0 + j)),
                  pl.BlockSpec((taps, tc), lambda i, j: (0, j)),
                  pl.BlockSpec((1, tc), lambda i, j: (0, j))],
        out_specs=pl.BlockSpec((tl, tc), lambda i, j: (i, j)),
        out_shape=jax.ShapeDtypeStruct((t, width), BF16),
        scratch_shapes=[pltpu.VMEM((tl + 2 * SUBLANES, tc), F32)],
        compiler_params=_cparams("parallel", "parallel"),
        name="ssd_dwconv_silu",
    )(src, src, src, conv_w, conv_b.reshape(1, width))


def _ssd_prep_kernel(raw_ref, bias_ref, alog_ref, dtT_ref, acol_ref, aT_ref, w_ref, *, heads):
    c = SSD_CHUNK
    lane = lax.broadcasted_iota(jnp.int32, (c, LANES), 1)
    lane_row = lax.broadcasted_iota(jnp.int32, (1, LANES), 1)
    li = lax.broadcasted_iota(jnp.int32, (c, c), 0)
    si = lax.broadcasted_iota(jnp.int32, (c, c), 1)
    lower = (si <= li).astype(F32)
    upper = (si >= li).astype(F32)
    a_neg = -jnp.exp(alog_ref[...])
    for i in range(raw_ref.shape[0] // c):
        rows = slice(i * c, (i + 1) * c)
        v = raw_ref[rows, :] + bias_ref[...]
        dt = jnp.maximum(v, 0.0) + jnp.log1p(jnp.exp(-jnp.abs(v)))
        dt = jnp.where(lane < 2 * heads, dt, 0.0)
        dta = dt * a_neg
        pre = jnp.dot(lower, dta, precision=HIGHEST, preferred_element_type=F32)
        suf = jnp.dot(upper, dta, precision=HIGHEST, preferred_element_type=F32)
        acum = jnp.where(lane < heads, pre, suf)
        aend = jnp.where(lane_row < heads, acum[c - 1:c, :], acum[0:1, :])
        dtT_ref[rows, :] = dt.T
        acol_ref[rows, :] = acum
        aT_ref[rows, :] = acum.T
        w_ref[rows, :] = (dt * jnp.exp(aend - acum)).T


def _ssd_prep(src, colblk, dt_bias, a_log, heads):
    t = src.shape[0]
    c = SSD_CHUNK
    pad = LANES - 2 * heads
    bias = jnp.pad(dt_bias.reshape(1, 2 * heads), ((0, 0), (0, pad)))
    alog = jnp.pad(a_log.reshape(1, 2 * heads), ((0, 0), (0, pad)))
    rows = _tile(t, 4 * c)
    spec = pl.BlockSpec((rows, LANES), lambda i: (i, 0))
    vec = pl.BlockSpec((1, LANES), lambda i: (0, 0))
    return pl.pallas_call(
        functools.partial(_ssd_prep_kernel, heads=heads),
        grid=(t // rows,),
        in_specs=[pl.BlockSpec((rows, LANES), lambda i: (i, colblk)), vec, vec],
        out_specs=[spec] * 4,
        out_shape=[jax.ShapeDtypeStruct((t, LANES), F32)] * 4,
        compiler_params=_cparams("parallel"),
        name="ssd_prep",
    )(src, bias, alog)


def _ssd_scan_kernel(x_ref, b_ref, c_ref, dtT_ref, acol_ref, aT_ref, w_ref, h0_ref, y_ref, hout_ref, h_scr,
                     *, reverse, col0, groups, hpg, nsteps):
    c = SSD_CHUNK
    n = SSD_STATE
    p = SSD_HEAD_DIM
    bsz = x_ref.shape[0]
    s = pl.program_id(0)

    @pl.when(s == 0)
    def _():
        h_scr[...] = h0_ref[...]

    li = lax.broadcasted_iota(jnp.int32, (c, c), 0)
    si = lax.broadcasted_iota(jnp.int32, (c, c), 1)
    mask = (li <= si) if reverse else (li >= si)
    lo_half = lax.broadcasted_iota(jnp.int32, (c, LANES), 1) < p
    pairs = hpg // 2
    nt = (((1,), (1,)), ((), ()))

    def stream(b):
        acol = acol_ref[b]
        a_t = aT_ref[b]
        dt_t = dtT_ref[b]
        w_t = w_ref[b]
        eend = jnp.exp(acol[0:1, :] if reverse else acol[c - 1:c, :])
        shared = {}

        def group_piece(g):
            bg = b_ref[b, :, g * n:(g + 1) * n]
            cg = c_ref[b, :, g * n:(g + 1) * n]
            h = h_scr[b, g]
            shared[g] = (bg,
                         lax.dot_general(cg, bg, nt, preferred_element_type=F32),
                         h,
                         lax.dot_general(cg, h.astype(BF16), nt, preferred_element_type=F32))

        def pair_piece(g, pr):
            bg, cb, h, yoff = shared[g]
            lanes = slice((g * pairs + pr) * LANES, (g * pairs + pr + 1) * LANES)
            loc = slice(pr * LANES, (pr + 1) * LANES)
            xpb = x_ref[b, :, lanes]
            k0 = col0 + g * hpg + 2 * pr
            ys, abc = [], []
            for q in range(2):
                k = k0 + q
                abc.append(jnp.broadcast_to(acol[:, k:k + 1], (c, c)))
                dec = jnp.exp(jnp.where(mask, abc[q] - a_t[k:k + 1, :], -jnp.inf))
                m = (cb * dec * dt_t[k:k + 1, :]).astype(BF16)
                ys.append(jnp.dot(m, xpb, preferred_element_type=F32))
            e_pair = jnp.exp(jnp.where(lo_half, abc[0], abc[1]))
            y = jnp.where(lo_half, ys[0], ys[1]) + yoff[:, loc] * e_pair
            y_ref[b, :, lanes] = y.astype(y_ref.dtype)
            per_head = lambda v0, v1, width: jnp.concatenate(
                [jnp.broadcast_to(v0, (p, width)), jnp.broadcast_to(v1, (p, width))], axis=0)
            xw_t = (xpb.astype(F32).T * per_head(w_t[k0:k0 + 1, :], w_t[k0 + 1:k0 + 2, :], c)).astype(BF16)
            upd = jnp.dot(xw_t, bg, preferred_element_type=F32)
            h_scr[b, g, loc, :] = h[loc] * per_head(eend[:, k0:k0 + 1], eend[:, k0 + 1:k0 + 2], n) + upd

        pieces = []
        for g in range(groups):
            pieces.append(functools.partial(group_piece, g))
            pieces.extend(functools.partial(pair_piece, g, pr) for pr in range(pairs))
        return pieces

    for step in zip(*[stream(b) for b in range(bsz)]):
        for piece in step:
            piece()

    @pl.when(s == nsteps - 1)
    def _():
        hout_ref[...] = h_scr[...]


def _ssd_scan(act, prep, h0, seq, heads, reverse):
    t = act.shape[0]
    bsz = t // seq
    c, n, groups = SSD_CHUNK, SSD_STATE, SSD_GROUPS
    hpg = heads // groups
    width = heads * SSD_HEAD_DIM
    gn = groups * n
    nsteps = seq // c
    act3 = act.reshape(bsz, seq, act.shape[1])
    prep3 = [p.reshape(bsz, seq, LANES) for p in prep]

    def chunk(s):
        return (nsteps - 1 - s) if reverse else s

    small = pl.BlockSpec((bsz, c, LANES), lambda s: (0, chunk(s), 0))
    hspec = pl.BlockSpec(h0.shape, lambda s: (0, 0, 0, 0))
    y, h = pl.pallas_call(
        functools.partial(_ssd_scan_kernel, reverse=reverse, col0=heads if reverse else 0,
                          groups=groups, hpg=hpg, nsteps=nsteps),
        grid=(nsteps,),
        in_specs=[pl.BlockSpec((bsz, c, width), lambda s: (0, chunk(s), 0)),
                  pl.BlockSpec((bsz, c, gn), lambda s: (0, chunk(s), width // gn)),
                  pl.BlockSpec((bsz, c, gn), lambda s: (0, chunk(s), width // gn + 1)),
                  small, small, small, small, hspec],
        out_specs=[pl.BlockSpec((bsz, c, width), lambda s: (0, chunk(s), 0)), hspec],
        out_shape=[jax.ShapeDtypeStruct((bsz, seq, width), BF16),
                   jax.ShapeDtypeStruct(h0.shape, F32)],
        scratch_shapes=[pltpu.VMEM(h0.shape, F32)],
        compiler_params=_cparams("arbitrary"),
        name="ssd_scan_bwd" if reverse else "ssd_scan_fwd",
    )(act3, act3, act3, *prep3, h0)
    return y.reshape(t, width), h


def _ssd_out_kernel(yf_ref, yb_ref, x_ref, z_ref, d_ref, nw_ref, o_ref, *, groups, gw):
    y = yf_ref[...].astype(F32) + yb_ref[...].astype(F32) + d_ref[...] * x_ref[...].astype(F32)
    g = y * _silu(z_ref[...])
    for i in range(groups):
        sl = slice(i * gw, (i + 1) * gw)
        gi = g[:, sl]
        r = lax.rsqrt(jnp.mean(gi * gi, axis=-1, keepdims=True) + EPS)
        o_ref[:, sl] = (gi * r * nw_ref[:, sl]).astype(o_ref.dtype)


def _ssd_out(yf, yb, act, rest, d_exp, norm_w, tm):
    t, width = yf.shape
    gw = width // SSD_GROUPS
    row = pl.BlockSpec((tm, width), lambda i: (i, 0))
    vec = pl.BlockSpec((1, width), lambda i: (0, 0))
    return pl.pallas_call(
        functools.partial(_ssd_out_kernel, groups=SSD_GROUPS, gw=gw),
        grid=(t // tm,),
        in_specs=[row, row, row, row, vec, vec],
        out_specs=row,
        out_shape=jax.ShapeDtypeStruct((t, width), BF16),
        compiler_params=_cparams("parallel"),
        name="ssd_out",
    )(yf, yb, act, rest, d_exp.reshape(1, width), norm_w.reshape(1, width))


def _wout_kernel(f_ref, s_ref, w_ref, x_ref, g_ref, o_ref):
    k1 = f_ref.shape[1]
    acc = jnp.dot(f_ref[...], w_ref[:k1, :], preferred_element_type=F32)
    acc = acc + jnp.dot(s_ref[...], w_ref[k1:, :], preferred_element_type=F32)
    o_ref[...] = x_ref[...] + g_ref[0] * acc


def _wout(f_out, s_out, w, x2d, gate, seq, tm, tn):
    t, d = x2d.shape
    bsz = gate.shape[0]
    k1, k2 = f_out.shape[1], s_out.shape[1]
    per = seq // tm
    return pl.pallas_call(
        _wout_kernel,
        grid=(t // tm, d // tn),
        in_specs=[pl.BlockSpec((tm, k1), lambda i, j: (i, 0)),
                  pl.BlockSpec((tm, k2), lambda i, j: (i, 0)),
                  pl.BlockSpec((k1 + k2, tn), lambda i, j: (0, j)),
                  pl.BlockSpec((tm, tn), lambda i, j: (i, j)),
                  pl.BlockSpec((1, 1, tn), lambda i, j: (i // per, 0, j))],
        out_specs=pl.BlockSpec((tm, tn), lambda i, j: (i, j)),
        out_shape=jax.ShapeDtypeStruct((t, d), F32),
        compiler_params=_cparams("parallel", "arbitrary"),
        name="out_proj",
    )(f_out, s_out, w, x2d, gate.reshape(bsz, 1, d))


def _ffn_down_kernel(up_ref, main_ref, dn_ref, val_ref, cw_ref, cb_ref, w_ref, x_ref, g_ref, nw_ref, o_ref,
                     a0_scr, a1_scr, *, per, nk, nsteps):
    q = pl.program_id(0)
    qc = jnp.minimum(q, nsteps - 1)
    t = (qc // nk) % per
    kc = qc % nk
    km = jnp.maximum(q - 1, 0) % nk
    gw = GRID_W
    tm, tk = a0_scr.shape

    @pl.when(q == 0)
    def _():
        a1_scr[...] = jnp.zeros((tm, tk), BF16)

    @pl.when(km == 0)
    def _():
        o_ref[...] = jnp.zeros_like(o_ref)

    def conv_and_matmul(a_new, a_prev):
        col = lax.broadcasted_iota(jnp.int32, (gw, LANES), 0)
        not_first = col != 0
        not_last = col != gw - 1
        nrows = tm // gw

        def conv_piece(c, r):
            ls = slice(c * LANES, (c + 1) * LANES)
            cw = cw_ref[kc, :, ls]
            rows = slice(r * gw, (r + 1) * gw)
            if r == 0:
                above = jnp.where(t == 0, 0.0, up_ref[0, :, ls].astype(F32))
            else:
                above = main_ref[0, (r - 1) * gw:r * gw, ls].astype(F32)
            mid = main_ref[0, rows, ls].astype(F32)
            if r == nrows - 1:
                below = jnp.where(t == per - 1, 0.0, dn_ref[0, :, ls].astype(F32))
            else:
                below = main_ref[0, (r + 1) * gw:(r + 2) * gw, ls].astype(F32)
            p = [cw[dx:dx + 1] * above + cw[3 + dx:4 + dx] * mid + cw[6 + dx:7 + dx] * below
                 for dx in range(3)]
            left = jnp.where(not_first, pltpu.roll(p[0], 1, 0), 0.0)
            right = jnp.where(not_last, pltpu.roll(p[2], gw - 1, 0), 0.0)
            acc = p[1] + cb_ref[kc, :, ls] + left + right
            a_new[rows, ls] = (_silu(acc) * val_ref[0, rows, ls].astype(F32)).astype(BF16)

        def matmul_piece(n):
            ns = slice(n * FFN_DOWN_SLAB, (n + 1) * FFN_DOWN_SLAB)
            o_ref[:, ns] += jnp.dot(a_prev[...], w_ref[:, ns], preferred_element_type=F32)

        pieces = [(c, r) for c in range(tk // LANES) for r in range(nrows)]
        nslabs = o_ref.shape[1] // FFN_DOWN_SLAB
        per_slab = -(-len(pieces) // nslabs)
        for n in range(nslabs):
            matmul_piece(n)
            for c, r in pieces[n * per_slab:(n + 1) * per_slab]:
                conv_piece(c, r)

    conv_and_matmul(a0_scr, a1_scr)

    @pl.when(q < nsteps)
    def _():
        a1_scr[...] = a0_scr[...]

    @pl.when(jnp.logical_and(km == nk - 1, q > 0))
    def _():
        for r in range(tm // SUBLANES):
            rows = slice(r * SUBLANES, (r + 1) * SUBLANES)
            v = x_ref[rows, :] + g_ref[0] * o_ref[rows, :]
            s = lax.rsqrt(jnp.mean(v * v, axis=-1, keepdims=True) + EPS)
            o_ref[rows, :] = v * s * nw_ref[...]


def _ffn_down(gate, val, conv_w9, conv_b, w, x2d, g, norm_w, seq, tm):
    nk, t, tk = gate.shape
    d = x2d.shape[1]
    bsz = g.shape[0]
    per = seq // tm
    hb = tm // GRID_W
    last = t // GRID_W - 1
    nsteps = (t // tm) * nk
    conv_w3 = conv_w9.reshape(9, nk, tk).swapaxes(0, 1)
    conv_b3 = conv_b.reshape(nk, 1, tk)

    def conv_tile(q):
        qc = jnp.minimum(q, nsteps - 1)
        return qc // nk, qc % nk

    def mm_tile(q):
        qm = jnp.maximum(q - 1, 0)
        return qm // nk, qm % nk

    def main_map(q):
        i, k = conv_tile(q)
        return k, i, 0

    def up_map(q):
        i, k = conv_tile(q)
        return k, jnp.maximum(i * hb - 1, 0), 0

    def dn_map(q):
        i, k = conv_tile(q)
        return k, jnp.minimum((i + 1) * hb, last), 0

    return pl.pallas_call(
        functools.partial(_ffn_down_kernel, per=per, nk=nk, nsteps=nsteps),
        grid=(nsteps + 1,),
        in_specs=[pl.BlockSpec((1, GRID_W, tk), up_map),
                  pl.BlockSpec((1, tm, tk), main_map),
                  pl.BlockSpec((1, GRID_W, tk), dn_map),
                  pl.BlockSpec((1, tm, tk), main_map),
                  pl.BlockSpec((nk, 9, tk), lambda q: (0, 0, 0)),
                  pl.BlockSpec((nk, 1, tk), lambda q: (0, 0, 0)),
                  pl.BlockSpec((tk, d), lambda q: (mm_tile(q)[1], 0)),
                  pl.BlockSpec((tm, d), lambda q: (mm_tile(q)[0], 0)),
                  pl.BlockSpec((1, 1, d), lambda q: (mm_tile(q)[0] // per, 0, 0)),
                  pl.BlockSpec((1, d), lambda q: (0, 0))],
        out_specs=pl.BlockSpec((tm, d), lambda q: (mm_tile(q)[0], 0)),
        out_shape=jax.ShapeDtypeStruct((t, d), F32),
        scratch_shapes=[pltpu.VMEM((tm, tk), BF16), pltpu.VMEM((tm, tk), BF16)],
        compiler_params=_cparams("arbitrary"),
        name="ffn_conv_down",
    )(gate, gate, gate, val, conv_w3, conv_b3, w, x2d, g.reshape(bsz, 1, d), norm_w.reshape(1, d))


def _tile(n, pref):
    t = pref
    while n % t:
        t //= 2
    return t


def _ssd_branch_inputs(h, w_rest, conv_w, conv_b, dt_bias, a_log, seq, fw, heads, tm):
    width = heads * SSD_HEAD_DIM
    conv_dim = conv_w.shape[1]
    rest = _mm_nt(h, w_rest, tm, IN_PROJ_TILE, F32, "in_proj_ssd")
    act = _dwconv_silu(rest, width, conv_dim, conv_w, conv_b, seq, _tile(seq, 512), 512)
    prep = _ssd_prep(rest, (width + conv_dim) // LANES, dt_bias, a_log, heads)
    return rest, act, prep


def kernel(x, c, ctx, c_ctx, ada_w, ada_b, norm_mix_w, w_in, fnet_w, ssd_conv_w, ssd_conv_b, ssd_dt_bias,
           ssd_a_log, ssd_d, ssd_norm_w, w_out, norm_ffn_w, ffn_w_gate, ffn_w_val, ffn_conv_w, ffn_conv_b,
           ffn_w_down, norm_final_w):
    bsz, seq, d = x.shape
    ctx_len = ctx.shape[1]
    depth = ada_w.shape[0]
    fw = fnet_w.shape[1] * fnet_w.shape[2]
    width = d - fw
    heads = width // SSD_HEAD_DIM
    conv_dim = ssd_conv_w.shape[2]
    assert depth == 1, "only the single-layer block is implemented"
    assert 2 * heads <= LANES and heads % SSD_GROUPS == 0 and (heads // SSD_GROUPS) % 2 == 0
    t = bsz * seq
    layer = 0

    cvecs = jnp.zeros((SUBLANES, d), F32).at[:bsz].set(c).at[bsz].set(c_ctx)
    mod = _adaln(cvecs, ada_w[layer], ada_b[layer])
    sh_m, sc_m, g_m, sh_f, sc_f, g_f = [mod[:bsz, i * d:(i + 1) * d] for i in range(6)]
    csh_m, csc_m = [jnp.broadcast_to(mod[bsz:bsz + 1, i * d:(i + 1) * d], (bsz, d)) for i in range(2)]

    w_in_t = jnp.swapaxes(w_in[layer], 0, 1)
    n_rest = w_in_t.shape[0] - fw
    w_u = _cast_pad(w_in_t, 0, fw, fw, d, MXU_WIDTH, d)
    w_rest = _cast_pad(w_in_t, fw, n_rest, n_rest + (-n_rest % IN_PROJ_TILE), d, MXU_WIDTH, d)
    conv_p = (ssd_conv_w[layer], ssd_conv_b[layer], ssd_dt_bias[layer], ssd_a_log[layer])

    x2d = x.reshape(t, d)
    hc = _lnmod(ctx.reshape(bsz * ctx_len, d), norm_mix_w[layer], csh_m, csc_m, ctx_len, ctx_len)
    _, act_c, prep_c = _ssd_branch_inputs(hc, w_rest, *conv_p, ctx_len, fw, heads, _tile(bsz * ctx_len, 1024))
    h_zero = jnp.zeros((bsz, SSD_GROUPS, (heads // SSD_GROUPS) * SSD_HEAD_DIM, SSD_STATE), F32)
    _, hc_f = _ssd_scan(act_c, prep_c, h_zero, ctx_len, heads, False)
    _, hc_b = _ssd_scan(act_c, prep_c, h_zero, ctx_len, heads, True)

    tm = _tile(seq, 1024)
    tr = _tile(seq, 256)
    hx = _lnmod(x2d, norm_mix_w[layer], sh_m, sc_m, seq, tr)
    u = _mm_nt(hx, w_u, tm, 512, F32, "in_proj_fnet")
    f_out = _fourier_mix(u, fnet_w[layer], bsz, seq)
    rest, act, prep = _ssd_branch_inputs(hx, w_rest, *conv_p, seq, fw, heads, tm)
    y_f, _ = _ssd_scan(act, prep, hc_f, seq, heads, False)
    y_b, _ = _ssd_scan(act, prep, hc_b, seq, heads, True)
    d_exp = jnp.repeat(ssd_d[layer], SSD_HEAD_DIM)
    s_out = _ssd_out(y_f, y_b, act, rest, d_exp, ssd_norm_w[layer], tr)
    x1 = _wout(f_out, s_out, _cast_pad(w_out[layer], 0, d, d, d, FFN_TILE, d), x2d, g_m, seq, tm, 1024)

    hx2 = _lnmod(x1, norm_ffn_w[layer], sh_f, sc_f, seq, tr)
    f = ffn_w_gate.shape[2]
    fpad = -f % FFN_TILE
    w_gate = _cast_pad(ffn_w_gate[layer], 0, d, d, f + fpad, d, FFN_TILE)
    w_val = _cast_pad(ffn_w_val[layer], 0, d, d, f + fpad, d, FFN_TILE)
    w_down = _cast_pad(ffn_w_down[layer], 0, f, f + fpad, d, FFN_TILE, d)
    gate, val = _mm_pair(hx2, w_gate, w_val, tm, FFN_TILE, "ffn_up")
    conv_w9 = jnp.pad(ffn_conv_w[layer].reshape(9, f), ((0, 0), (0, fpad)))
    out = _ffn_down(gate, val, conv_w9, jnp.pad(ffn_conv_b[layer], (0, fpad)),
                    w_down, x1, g_f, norm_final_w, seq, _tile(seq, 512))
    return out.reshape(bsz, seq, d)
```

```python
import functools
import math

import numpy as np
import jax
import jax.numpy as jnp
from jax import lax
from jax.experimental import pallas as pl
from jax.experimental.pallas import tpu as pltpu

F32 = jnp.float32
BF16 = jnp.bfloat16
HIGHEST = lax.Precision.HIGHEST

GRID_W = 64
SSD_HEAD_DIM = 64
SSD_GROUPS = 8
SSD_STATE = 128
SSD_CHUNK = 128
EPS = 1e-6

LANES = 128
SUBLANES = 8
MXU_WIDTH = 256
IN_PROJ_TILE = 768
VMEM_LIMIT = 56 * 1024 * 1024
FFN_TILE = 512
FFN_DOWN_SLAB = 512


def _cparams(*sem):
    return pltpu.CompilerParams(dimension_semantics=sem, vmem_limit_bytes=VMEM_LIMIT)


def _silu(v):
    return v * jax.nn.sigmoid(v)


def _adaln_kernel(c_ref, w_ref, b_ref, o_ref):
    c = c_ref[...]
    s = _silu(c).astype(BF16)
    o_ref[...] = jnp.dot(s, w_ref[...].astype(BF16), preferred_element_type=F32) + b_ref[...]


def _adaln(cvecs, w, b):
    d, n = w.shape
    tn = 512
    return pl.pallas_call(
        _adaln_kernel,
        grid=(n // tn,),
        in_specs=[pl.BlockSpec((SUBLANES, d), lambda j: (0, 0)),
                  pl.BlockSpec((d, tn), lambda j: (0, j)),
                  pl.BlockSpec((1, tn), lambda j: (0, j))],
        out_specs=pl.BlockSpec((SUBLANES, tn), lambda j: (0, j)),
        out_shape=jax.ShapeDtypeStruct((SUBLANES, n), F32),
        compiler_params=_cparams("parallel"),
        name="adaln",
    )(cvecs, w, b.reshape(1, n))


def _lnmod_kernel(x_ref, w_ref, sh_ref, sc_ref, o_ref):
    x = x_ref[...]
    r = lax.rsqrt(jnp.mean(x * x, axis=-1, keepdims=True) + EPS)
    o_ref[...] = ((x * r) * w_ref[...] * (1.0 + sc_ref[0]) + sh_ref[0]).astype(o_ref.dtype)


def _lnmod(x2d, w, shift, scale, seq, tm):
    t, d = x2d.shape
    bsz = shift.shape[0]
    per = seq // tm
    return pl.pallas_call(
        _lnmod_kernel,
        grid=(t // tm,),
        in_specs=[pl.BlockSpec((tm, d), lambda i: (i, 0)),
                  pl.BlockSpec((1, d), lambda i: (0, 0)),
                  pl.BlockSpec((1, 1, d), lambda i: (i // per, 0, 0)),
                  pl.BlockSpec((1, 1, d), lambda i: (i // per, 0, 0))],
        out_specs=pl.BlockSpec((tm, d), lambda i: (i, 0)),
        out_shape=jax.ShapeDtypeStruct((t, d), BF16),
        compiler_params=_cparams("parallel"),
        name="lnmod",
    )(x2d, w.reshape(1, d), shift.reshape(bsz, 1, d), scale.reshape(bsz, 1, d))


def _cast_pad_kernel(w_ref, o_ref, *, rows, cols, tr, tc):
    r = lax.broadcasted_iota(jnp.int32, (tr, tc), 0) + pl.program_id(0) * tr
    c = lax.broadcasted_iota(jnp.int32, (tr, tc), 1) + pl.program_id(1) * tc
    o_ref[...] = jnp.where(jnp.logical_and(r < rows, c < cols), w_ref[...], 0.0).astype(o_ref.dtype)


def _cast_pad(w, row0, rows, out_rows, out_cols, tr, tc):
    cols = w.shape[1]
    assert row0 % tr == 0 and out_rows % tr == 0 and out_cols % tc == 0
    rb0 = row0 // tr
    return pl.pallas_call(
        functools.partial(_cast_pad_kernel, rows=rows, cols=cols, tr=tr, tc=tc),
        grid=(out_rows // tr, out_cols // tc),
        in_specs=[pl.BlockSpec((tr, tc), lambda i, j: (rb0 + i, j))],
        out_specs=pl.BlockSpec((tr, tc), lambda i, j: (i, j)),
        out_shape=jax.ShapeDtypeStruct((out_rows, out_cols), BF16),
        compiler_params=_cparams("parallel", "parallel"),
        name="weight_cast",
    )(w)


def _mm_nt_kernel(a_ref, wt_ref, o_ref):
    o_ref[...] = lax.dot_general(a_ref[...], wt_ref[...], (((1,), (1,)), ((), ())),
                                 preferred_element_type=F32).astype(o_ref.dtype)


def _mm_nt(a, wt, tm, tn, out_dtype, name):
    m, k = a.shape
    n = wt.shape[0]
    return pl.pallas_call(
        _mm_nt_kernel,
        grid=(m // tm, n // tn),
        in_specs=[pl.BlockSpec((tm, k), lambda i, j: (i, 0)),
                  pl.BlockSpec((tn, k), lambda i, j: (j, 0))],
        out_specs=pl.BlockSpec((tm, tn), lambda i, j: (i, j)),
        out_shape=jax.ShapeDtypeStruct((m, n), out_dtype),
        compiler_params=_cparams("parallel", "arbitrary"),
        name=name,
    )(a, wt)


def _mm_pair_kernel(a_ref, w1_ref, w2_ref, o1_ref, o2_ref):
    a = a_ref[...]
    o1_ref[0] = jnp.dot(a, w1_ref[...], preferred_element_type=F32).astype(o1_ref.dtype)
    o2_ref[0] = jnp.dot(a, w2_ref[...], preferred_element_type=F32).astype(o2_ref.dtype)


def _mm_pair(a, w1, w2, tm, tn, name):
    m, k = a.shape
    n = w1.shape[1]
    out = jax.ShapeDtypeStruct((n // tn, m, tn), BF16)
    return pl.pallas_call(
        _mm_pair_kernel,
        grid=(m // tm, n // tn),
        in_specs=[pl.BlockSpec((tm, k), lambda i, j: (i, 0)),
                  pl.BlockSpec((k, tn), lambda i, j: (0, j)),
                  pl.BlockSpec((k, tn), lambda i, j: (0, j))],
        out_specs=[pl.BlockSpec((1, tm, tn), lambda i, j: (j, i, 0)),
                   pl.BlockSpec((1, tm, tn), lambda i, j: (j, i, 0))],
        out_shape=[out, out],
        compiler_params=_cparams("parallel", "arbitrary"),
        name=name,
    )(a, w1, w2)


def _dft_tables(n):
    idx = np.arange(n)
    ang = 2.0 * np.pi * ((idx[:, None] * idx[None, :]) % n) / n
    return np.cos(ang), np.sin(ang)


def _fold_kernel(cs_ref, wf_ref, o_ref):
    o_ref[0] = jnp.dot(cs_ref[...], wf_ref[0], precision=HIGHEST,
                       preferred_element_type=F32).astype(o_ref.dtype)


def _fourier_fold(fnet_w, seq):
    g, c, _ = fnet_w.shape
    cc, sc = _dft_tables(c)
    cs = jnp.asarray(np.concatenate([cc, sc], axis=0) / math.sqrt(seq * c), F32)
    return pl.pallas_call(
        _fold_kernel,
        grid=(g,),
        in_specs=[pl.BlockSpec((2 * c, c), lambda i: (0, 0)),
                  pl.BlockSpec((1, c, c), lambda i: (i, 0, 0))],
        out_specs=pl.BlockSpec((1, 2 * c, c), lambda i: (i, 0, 0)),
        out_shape=jax.ShapeDtypeStruct((g, 2 * c, c), BF16),
        compiler_params=_cparams("parallel"),
        name="fnet_fold",
    )(cs, fnet_w)


def _dft1_kernel(f_ref, u_ref, zr_ref, zi_ref):
    n1 = zr_ref.shape[1]
    f = f_ref[...]
    for j in range(u_ref.shape[2]):
        z = jnp.dot(f, u_ref[0, :, j, :].astype(BF16), preferred_element_type=F32)
        zr_ref[0, :, j, :] = z[:n1]
        zi_ref[0, :, j, :] = z[n1:]


def _dft_stage1(u4):
    bsz, n1, n2, wdt = u4.shape
    c1, s1 = _dft_tables(n1)
    f1 = jnp.asarray(np.concatenate([c1, -s1], axis=0), BF16)
    spec = pl.BlockSpec((1, n1, SUBLANES, wdt), lambda b, j: (b, 0, j, 0))
    return pl.pallas_call(
        _dft1_kernel,
        grid=(bsz, n2 // SUBLANES),
        in_specs=[pl.BlockSpec((2 * n1, n1), lambda b, j: (0, 0)), spec],
        out_specs=[spec, spec],
        out_shape=[jax.ShapeDtypeStruct(u4.shape, F32)] * 2,
        compiler_params=_cparams("parallel", "parallel"),
        name="fnet_dft_stage1",
    )(f1, u4)


def _dft2_kernel(c2_ref, s2_ref, tc_ref, ts_ref, wcs_ref, zr_ref, zi_ref, o_ref, *, kb, groups, cg):
    c2 = c2_ref[...]
    s2 = s2_ref[...]
    for j in range(kb):
        tcj = tc_ref[j:j + 1, :]
        tsj = ts_ref[j:j + 1, :]
        gc = c2 * tcj - s2 * tsj
        gs = s2 * tcj + c2 * tsj
        gmat = jnp.concatenate([jnp.concatenate([gc, gs], axis=1),
                                jnp.concatenate([-gs, gc], axis=1)], axis=0).astype(BF16)
        z = jnp.concatenate([zr_ref[0, j], zi_ref[0, j]], axis=0).astype(BF16)
        p = jnp.dot(gmat, z, preferred_element_type=F32)
        n2 = c2.shape[0]
        pr = p[:n2].astype(BF16)
        pi = p[n2:].astype(BF16)
        for g in range(groups):
            sl = slice(g * cg, (g + 1) * cg)
            y = jnp.dot(jnp.concatenate([pr[:, sl], pi[:, sl]], axis=1), wcs_ref[g],
                        preferred_element_type=F32)
            o_ref[0, :, j, sl] = y.astype(o_ref.dtype)


def _dft_stage2(zr4, zi4, wcs, seq):
    bsz, n1, n2, wdt = zr4.shape
    groups, _, cg = wcs.shape
    c2, s2 = _dft_tables(n2)
    k1 = np.arange(n1)[:, None]
    l2 = np.arange(n2)[None, :]
    ang = 2.0 * np.pi * ((k1 * l2) % seq) / seq
    kb = SUBLANES
    zspec = pl.BlockSpec((1, kb, n2, wdt), lambda b, i: (b, i, 0, 0))
    full = lambda shape: pl.BlockSpec(shape, lambda b, i: (0,) * len(shape))
    return pl.pallas_call(
        functools.partial(_dft2_kernel, kb=kb, groups=groups, cg=cg),
        grid=(bsz, n1 // kb),
        in_specs=[full((n2, n2)), full((n2, n2)),
                  pl.BlockSpec((kb, n2), lambda b, i: (i, 0)),
                  pl.BlockSpec((kb, n2), lambda b, i: (i, 0)),
                  full(wcs.shape), zspec, zspec],
        out_specs=pl.BlockSpec((1, n2, kb, wdt), lambda b, i: (b, 0, i, 0)),
        out_shape=jax.ShapeDtypeStruct((bsz, n2, n1, wdt), BF16),
        compiler_params=_cparams("parallel", "parallel"),
        name="fnet_dft_stage2",
    )(jnp.asarray(c2, F32), jnp.asarray(s2, F32), jnp.asarray(np.cos(ang), F32),
      jnp.asarray(np.sin(ang), F32), wcs, zr4, zi4)


def _fourier_mix(u, fnet_w, bsz, seq):
    wdt = u.shape[1]
    n2 = LANES
    n1 = seq // n2
    wcs = _fourier_fold(fnet_w, seq)
    zr, zi = _dft_stage1(u.reshape(bsz, n1, n2, wdt))
    out = _dft_stage2(zr, zi, wcs, seq)
    return out.reshape(bsz * seq, wdt)


def _dwconv_silu_kernel(prev_ref, main_ref, next_ref, w_ref, b_ref, o_ref, *, per, tl, taps):
    t = pl.program_id(0) % per
    h = SUBLANES
    half = (taps - 1) // 2
    x = main_ref[...]
    acc = b_ref[...] + w_ref[half:half + 1, :] * x
    for k in range(taps):
        if k != half:
            acc = acc + w_ref[k:k + 1, :] * pltpu.roll(x, (half - k) % tl, 0)
    o_ref[...] = _silu(acc).astype(o_ref.dtype)

    def edge(win):
        e = b_ref[...] + w_ref[0:1, :] * win[h - half:2 * h - half]
        for k in range(1, taps):
            e = e + w_ref[k:k + 1, :] * win[h - half + k:2 * h - half + k]
        return _silu(e).astype(o_ref.dtype)

    top = jnp.where(t == 0, 0.0, prev_ref[...])
    o_ref[0:h, :] = edge(jnp.concatenate([top, x[0:2 * h]], axis=0))
    bot = jnp.where(t == per - 1, 0.0, next_ref[...])
    o_ref[tl - h:tl, :] = edge(jnp.concatenate([x[tl - 2 * h:tl], bot], axis=0))


def _dwconv_silu(src, col0, width, conv_w, conv_b, seq, tl, tc):
    t = src.shape[0]
    taps = conv_w.shape[0]
    per = seq // tl
    cb0 = col0 // tc
    hb = tl // SUBLANES
    last = t // SUBLANES - 1
    return pl.pallas_call(
        functools.partial(_dwconv_silu_kernel, per=per, tl=tl, taps=taps),
        grid=(t // tl, width // tc),
        in_specs=[pl.BlockSpec((SUBLANES, tc), lambda i, j: (jnp.maximum(i * hb - 1, 0), cb0 + j)),
                  pl.BlockSpec((tl, tc), lambda i, j: (i, cb0 + j)),
                  pl.BlockSpec((SUBLANES, tc), lambda i, j: (jnp.minimum((i + 1) * hb, last), cb0 + j)),
                  pl.BlockSpec((taps, tc), lambda i, j: (0, j)),
                  pl.BlockSpec((1, tc), lambda i, j: (0, j))],
        out_specs=pl.BlockSpec((tl, tc), lambda i, j: (i, j)),
        out_shape=jax.ShapeDtypeStruct((t, width), BF16),
        compiler_params=_cparams("parallel", "parallel"),
        name="ssd_dwconv_silu",
    )(src, src, src, conv_w, conv_b.reshape(1, width))


def _ssd_prep_kernel(raw_ref, bias_ref, alog_ref, dtT_ref, acol_ref, aT_ref, w_ref, *, heads):
    c = SSD_CHUNK
    lane = lax.broadcasted_iota(jnp.int32, (c, LANES), 1)
    lane_row = lax.broadcasted_iota(jnp.int32, (1, LANES), 1)
    li = lax.broadcasted_iota(jnp.int32, (c, c), 0)
    si = lax.broadcasted_iota(jnp.int32, (c, c), 1)
    lower = (si <= li).astype(F32)
    upper = (si >= li).astype(F32)
    a_neg = -jnp.exp(alog_ref[...])
    for i in range(raw_ref.shape[0] // c):
        rows = slice(i * c, (i + 1) * c)
        v = raw_ref[rows, :] + bias_ref[...]
        dt = jnp.maximum(v, 0.0) + jnp.log1p(jnp.exp(-jnp.abs(v)))
        dt = jnp.where(lane < 2 * heads, dt, 0.0)
        dta = dt * a_neg
        pre = jnp.dot(lower, dta, precision=HIGHEST, preferred_element_type=F32)
        suf = jnp.dot(upper, dta, precision=HIGHEST, preferred_element_type=F32)
        acum = jnp.where(lane < heads, pre, suf)
        aend = jnp.where(lane_row < heads, acum[c - 1:c, :], acum[0:1, :])
        dtT_ref[rows, :] = dt.T
        acol_ref[rows, :] = acum
        aT_ref[rows, :] = acum.T
        w_ref[rows, :] = (dt * jnp.exp(aend - acum)).T


def _ssd_prep(src, colblk, dt_bias, a_log, heads):
    t = src.shape[0]
    c = SSD_CHUNK
    pad = LANES - 2 * heads
    bias = jnp.pad(dt_bias.reshape(1, 2 * heads), ((0, 0), (0, pad)))
    alog = jnp.pad(a_log.reshape(1, 2 * heads), ((0, 0), (0, pad)))
    rows = _tile(t, 4 * c)
    spec = pl.BlockSpec((rows, LANES), lambda i: (i, 0))
    vec = pl.BlockSpec((1, LANES), lambda i: (0, 0))
    return pl.pallas_call(
        functools.partial(_ssd_prep_kernel, heads=heads),
        grid=(t // rows,),
        in_specs=[pl.BlockSpec((rows, LANES), lambda i: (i, colblk)), vec, vec],
        out_specs=[spec] * 4,
        out_shape=[jax.ShapeDtypeStruct((t, LANES), F32)] * 4,
        compiler_params=_cparams("parallel"),
        name="ssd_prep",
    )(src, bias, alog)


def _ssd_scan_kernel(x_ref, b_ref, c_ref, dtT_ref, acol_ref, aT_ref, w_ref, h0_ref, y_ref, hout_ref, h_scr,
                     *, reverse, col0, groups, hpg, nsteps):
    c = SSD_CHUNK
    n = SSD_STATE
    p = SSD_HEAD_DIM
    bsz = x_ref.shape[0]
    s = pl.program_id(0)

    @pl.when(s == 0)
    def _():
        h_scr[...] = h0_ref[...]

    li = lax.broadcasted_iota(jnp.int32, (c, c), 0)
    si = lax.broadcasted_iota(jnp.int32, (c, c), 1)
    mask = (li <= si) if reverse else (li >= si)
    lo_half = lax.broadcasted_iota(jnp.int32, (c, LANES), 1) < p
    pairs = hpg // 2
    nt = (((1,), (1,)), ((), ()))

    def stream(b):
        acol = acol_ref[b]
        row = lambda ref, k: ref[b, k:k + 1, :]
        eend = jnp.exp(acol[0:1, :] if reverse else acol[c - 1:c, :])
        shared = {}

        def group_piece(g):
            bg = b_ref[b, :, g * n:(g + 1) * n]
            cg = c_ref[b, :, g * n:(g + 1) * n]
            h = h_scr[b, g]
            shared[g] = (bg,
                         lax.dot_general(cg, bg, nt, preferred_element_type=F32),
                         h,
                         lax.dot_general(cg, h.astype(BF16), nt, preferred_element_type=F32))

        def pair_piece(g, pr):
            bg, cb, h, yoff = shared[g]
            lanes = slice((g * pairs + pr) * LANES, (g * pairs + pr + 1) * LANES)
            loc = slice(pr * LANES, (pr + 1) * LANES)
            xpb = x_ref[b, :, lanes]
            k0 = col0 + g * hpg + 2 * pr
            ys, abc = [], []
            for q in range(2):
                k = k0 + q
                abc.append(jnp.broadcast_to(acol[:, k:k + 1], (c, c)))
                dec = jnp.exp(jnp.where(mask, abc[q] - row(aT_ref, k), -jnp.inf))
                m = (cb * dec * row(dtT_ref, k)).astype(BF16)
                ys.append(jnp.dot(m, xpb, preferred_element_type=F32))
            e_pair = jnp.exp(jnp.where(lo_half, abc[0], abc[1]))
            y = jnp.where(lo_half, ys[0], ys[1]) + yoff[:, loc] * e_pair
            y_ref[b, :, lanes] = y.astype(y_ref.dtype)
            per_head = lambda v0, v1, width: jnp.concatenate(
                [jnp.broadcast_to(v0, (p, width)), jnp.broadcast_to(v1, (p, width))], axis=0)
            xw_t = (xpb.astype(F32).T * per_head(row(w_ref, k0), row(w_ref, k0 + 1), c)).astype(BF16)
            upd = jnp.dot(xw_t, bg, preferred_element_type=F32)
            h_scr[b, g, loc, :] = h[loc] * per_head(eend[:, k0:k0 + 1], eend[:, k0 + 1:k0 + 2], n) + upd

        pieces = []
        for g in range(groups):
            pieces.append(functools.partial(group_piece, g))
            pieces.extend(functools.partial(pair_piece, g, pr) for pr in range(pairs))
        return pieces

    for step in zip(*[stream(b) for b in range(bsz)]):
        for piece in step:
            piece()

    @pl.when(s == nsteps - 1)
    def _():
        hout_ref[...] = h_scr[...]


def _ssd_scan(act, prep, h0, seq, heads, reverse):
    t = act.shape[0]
    bsz = t // seq
    c, n, groups = SSD_CHUNK, SSD_STATE, SSD_GROUPS
    hpg = heads // groups
    width = heads * SSD_HEAD_DIM
    gn = groups * n
    nsteps = seq // c
    act3 = act.reshape(bsz, seq, act.shape[1])
    prep3 = [p.reshape(bsz, seq, LANES) for p in prep]

    def chunk(s):
        return (nsteps - 1 - s) if reverse else s

    small = pl.BlockSpec((bsz, c, LANES), lambda s: (0, chunk(s), 0))
    hspec = pl.BlockSpec(h0.shape, lambda s: (0, 0, 0, 0))
    y, h = pl.pallas_call(
        functools.partial(_ssd_scan_kernel, reverse=reverse, col0=heads if reverse else 0,
                          groups=groups, hpg=hpg, nsteps=nsteps),
        grid=(nsteps,),
        in_specs=[pl.BlockSpec((bsz, c, width), lambda s: (0, chunk(s), 0)),
                  pl.BlockSpec((bsz, c, gn), lambda s: (0, chunk(s), width // gn)),
                  pl.BlockSpec((bsz, c, gn), lambda s: (0, chunk(s), width // gn + 1)),
                  small, small, small, small, hspec],
        out_specs=[pl.BlockSpec((bsz, c, width), lambda s: (0, chunk(s), 0)), hspec],
        out_shape=[jax.ShapeDtypeStruct((bsz, seq, width), BF16),
                   jax.ShapeDtypeStruct(h0.shape, F32)],
        scratch_shapes=[pltpu.VMEM(h0.shape, F32)],
        compiler_params=_cparams("arbitrary"),
        name="ssd_scan_bwd" if reverse else "ssd_scan_fwd",
    )(act3, act3, act3, *prep3, h0)
    return y.reshape(t, width), h


def _ssd_out_kernel(yf_ref, yb_ref, x_ref, z_ref, d_ref, nw_ref, o_ref, *, groups, gw):
    y = yf_ref[...].astype(F32) + yb_ref[...].astype(F32) + d_ref[...] * x_ref[...].astype(F32)
    g = y * _silu(z_ref[...])
    for i in range(groups):
        sl = slice(i * gw, (i + 1) * gw)
        gi = g[:, sl]
        r = lax.rsqrt(jnp.mean(gi * gi, axis=-1, keepdims=True) + EPS)
        o_ref[:, sl] = (gi * r * nw_ref[:, sl]).astype(o_ref.dtype)


def _ssd_out(yf, yb, act, rest, d_exp, norm_w, tm):
    t, width = yf.shape
    gw = width // SSD_GROUPS
    row = pl.BlockSpec((tm, width), lambda i: (i, 0))
    vec = pl.BlockSpec((1, width), lambda i: (0, 0))
    return pl.pallas_call(
        functools.partial(_ssd_out_kernel, groups=SSD_GROUPS, gw=gw),
        grid=(t // tm,),
        in_specs=[row, row, row, row, vec, vec],
        out_specs=row,
        out_shape=jax.ShapeDtypeStruct((t, width), BF16),
        compiler_params=_cparams("parallel"),
        name="ssd_out",
    )(yf, yb, act, rest, d_exp.reshape(1, width), norm_w.reshape(1, width))


def _wout_kernel(f_ref, s_ref, w_ref, x_ref, g_ref, o_ref):
    k1 = f_ref.shape[1]
    acc = jnp.dot(f_ref[...], w_ref[:k1, :], preferred_element_type=F32)
    acc = acc + jnp.dot(s_ref[...], w_ref[k1:, :], preferred_element_type=F32)
    o_ref[...] = x_ref[...] + g_ref[0] * acc


def _wout(f_out, s_out, w, x2d, gate, seq, tm, tn):
    t, d = x2d.shape
    bsz = gate.shape[0]
    k1, k2 = f_out.shape[1], s_out.shape[1]
    per = seq // tm
    return pl.pallas_call(
        _wout_kernel,
        grid=(t // tm, d // tn),
        in_specs=[pl.BlockSpec((tm, k1), lambda i, j: (i, 0)),
                  pl.BlockSpec((tm, k2), lambda i, j: (i, 0)),
                  pl.BlockSpec((k1 + k2, tn), lambda i, j: (0, j)),
                  pl.BlockSpec((tm, tn), lambda i, j: (i, j)),
                  pl.BlockSpec((1, 1, tn), lambda i, j: (i // per, 0, j))],
        out_specs=pl.BlockSpec((tm, tn), lambda i, j: (i, j)),
        out_shape=jax.ShapeDtypeStruct((t, d), F32),
        compiler_params=_cparams("parallel", "arbitrary"),
        name="out_proj",
    )(f_out, s_out, w, x2d, gate.reshape(bsz, 1, d))


def _ffn_down_kernel(up_ref, main_ref, dn_ref, val_ref, cw_ref, cb_ref, w_ref, x_ref, g_ref, nw_ref, o_ref,
                     a0_scr, a1_scr, *, per, nk, nsteps):
    q = pl.program_id(0)
    qc = jnp.minimum(q, nsteps - 1)
    t = (qc // nk) % per
    kc = qc % nk
    km = jnp.maximum(q - 1, 0) % nk
    gw = GRID_W
    tm, tk = a0_scr.shape

    @pl.when(q == 0)
    def _():
        a1_scr[...] = jnp.zeros((tm, tk), BF16)

    @pl.when(km == 0)
    def _():
        o_ref[...] = jnp.zeros_like(o_ref)

    def conv_and_matmul(a_new, a_prev):
        col = lax.broadcasted_iota(jnp.int32, (gw, LANES), 0)
        not_first = col != 0
        not_last = col != gw - 1
        nrows = tm // gw

        def conv_piece(c, r):
            ls = slice(c * LANES, (c + 1) * LANES)
            cw = cw_ref[kc, :, ls]
            rows = slice(r * gw, (r + 1) * gw)
            if r == 0:
                above = jnp.where(t == 0, 0.0, up_ref[0, :, ls].astype(F32))
            else:
                above = main_ref[0, (r - 1) * gw:r * gw, ls].astype(F32)
            mid = main_ref[0, rows, ls].astype(F32)
            if r == nrows - 1:
                below = jnp.where(t == per - 1, 0.0, dn_ref[0, :, ls].astype(F32))
            else:
                below = main_ref[0, (r + 1) * gw:(r + 2) * gw, ls].astype(F32)
            p = [cw[dx:dx + 1] * above + cw[3 + dx:4 + dx] * mid + cw[6 + dx:7 + dx] * below
                 for dx in range(3)]
            left = jnp.where(not_first, pltpu.roll(p[0], 1, 0), 0.0)
            right = jnp.where(not_last, pltpu.roll(p[2], gw - 1, 0), 0.0)
            acc = p[1] + cb_ref[kc, :, ls] + left + right
            a_new[rows, ls] = (_silu(acc) * val_ref[0, rows, ls].astype(F32)).astype(BF16)

        def matmul_piece(n):
            ns = slice(n * FFN_DOWN_SLAB, (n + 1) * FFN_DOWN_SLAB)
            o_ref[:, ns] += jnp.dot(a_prev[...], w_ref[:, ns], preferred_element_type=F32)

        pieces = [(c, r) for c in range(tk // LANES) for r in range(nrows)]
        nslabs = o_ref.shape[1] // FFN_DOWN_SLAB
        per_slab = -(-len(pieces) // nslabs)
        for n in range(nslabs):
            matmul_piece(n)
            for c, r in pieces[n * per_slab:(n + 1) * per_slab]:
                conv_piece(c, r)

    conv_and_matmul(a0_scr, a1_scr)

    @pl.when(q < nsteps)
    def _():
        a1_scr[...] = a0_scr[...]

    @pl.when(jnp.logical_and(km == nk - 1, q > 0))
    def _():
        for r in range(tm // SUBLANES):
            rows = slice(r * SUBLANES, (r + 1) * SUBLANES)
            v = x_ref[rows, :] + g_ref[0] * o_ref[rows, :]
            s = lax.rsqrt(jnp.mean(v * v, axis=-1, keepdims=True) + EPS)
            o_ref[rows, :] = v * s * nw_ref[...]


def _ffn_down(gate, val, conv_w9, conv_b, w, x2d, g, norm_w, seq, tm):
    nk, t, tk = gate.shape
    d = x2d.shape[1]
    bsz = g.shape[0]
    per = seq // tm
    hb = tm // GRID_W
    last = t // GRID_W - 1
    nsteps = (t // tm) * nk
    conv_w3 = conv_w9.reshape(9, nk, tk).swapaxes(0, 1)
    conv_b3 = conv_b.reshape(nk, 1, tk)

    def conv_tile(q):
        qc = jnp.minimum(q, nsteps - 1)
        return qc // nk, qc % nk

    def mm_tile(q):
        qm = jnp.maximum(q - 1, 0)
        return qm // nk, qm % nk

    def main_map(q):
        i, k = conv_tile(q)
        return k, i, 0

    def up_map(q):
        i, k = conv_tile(q)
        return k, jnp.maximum(i * hb - 1, 0), 0

    def dn_map(q):
        i, k = conv_tile(q)
        return k, jnp.minimum((i + 1) * hb, last), 0

    return pl.pallas_call(
        functools.partial(_ffn_down_kernel, per=per, nk=nk, nsteps=nsteps),
        grid=(nsteps + 1,),
        in_specs=[pl.BlockSpec((1, GRID_W, tk), up_map),
                  pl.BlockSpec((1, tm, tk), main_map),
                  pl.BlockSpec((1, GRID_W, tk), dn_map),
                  pl.BlockSpec((1, tm, tk), main_map),
                  pl.BlockSpec((nk, 9, tk), lambda q: (0, 0, 0)),
                  pl.BlockSpec((nk, 1, tk), lambda q: (0, 0, 0)),
                  pl.BlockSpec((tk, d), lambda q: (mm_tile(q)[1], 0)),
                  pl.BlockSpec((tm, d), lambda q: (mm_tile(q)[0], 0)),
                  pl.BlockSpec((1, 1, d), lambda q: (mm_tile(q)[0] // per, 0, 0)),
                  pl.BlockSpec((1, d), lambda q: (0, 0))],
        out_specs=pl.BlockSpec((tm, d), lambda q: (mm_tile(q)[0], 0)),
        out_shape=jax.ShapeDtypeStruct((t, d), F32),
        scratch_shapes=[pltpu.VMEM((tm, tk), BF16), pltpu.VMEM((tm, tk), BF16)],
        compiler_params=_cparams("arbitrary"),
        name="ffn_conv_down",
    )(gate, gate, gate, val, conv_w3, conv_b3, w, x2d, g.reshape(bsz, 1, d), norm_w.reshape(1, d))


def _tile(n, pref):
    t = pref
    while n % t:
        t //= 2
    return t


def _ssd_branch_inputs(h, w_rest, conv_w, conv_b, dt_bias, a_log, seq, fw, heads, tm):
    width = heads * SSD_HEAD_DIM
    conv_dim = conv_w.shape[1]
    rest = _mm_nt(h, w_rest, tm, IN_PROJ_TILE, F32, "in_proj_ssd")
    act = _dwconv_silu(rest, width, conv_dim, conv_w, conv_b, seq, _tile(seq, 512), 512)
    prep = _ssd_prep(rest, (width + conv_dim) // LANES, dt_bias, a_log, heads)
    return rest, act, prep


def kernel(x, c, ctx, c_ctx, ada_w, ada_b, norm_mix_w, w_in, fnet_w, ssd_conv_w, ssd_conv_b, ssd_dt_bias,
           ssd_a_log, ssd_d, ssd_norm_w, w_out, norm_ffn_w, ffn_w_gate, ffn_w_val, ffn_conv_w, ffn_conv_b,
           ffn_w_down, norm_final_w):
    bsz, seq, d = x.shape
    ctx_len = ctx.shape[1]
    depth = ada_w.shape[0]
    fw = fnet_w.shape[1] * fnet_w.shape[2]
    width = d - fw
    heads = width // SSD_HEAD_DIM
    conv_dim = ssd_conv_w.shape[2]
    assert depth == 1, "only the single-layer block is implemented"
    assert 2 * heads <= LANES and heads % SSD_GROUPS == 0 and (heads // SSD_GROUPS) % 2 == 0
    t = bsz * seq
    layer = 0

    cvecs = jnp.zeros((SUBLANES, d), F32).at[:bsz].set(c).at[bsz].set(c_ctx)
    mod = _adaln(cvecs, ada_w[layer], ada_b[layer])
    sh_m, sc_m, g_m, sh_f, sc_f, g_f = [mod[:bsz, i * d:(i + 1) * d] for i in range(6)]
    csh_m, csc_m = [jnp.broadcast_to(mod[bsz:bsz + 1, i * d:(i + 1) * d], (bsz, d)) for i in range(2)]

    w_in_t = jnp.swapaxes(w_in[layer], 0, 1)
    n_rest = w_in_t.shape[0] - fw
    w_u = _cast_pad(w_in_t, 0, fw, fw, d, MXU_WIDTH, d)
    w_rest = _cast_pad(w_in_t, fw, n_rest, n_rest + (-n_rest % IN_PROJ_TILE), d, MXU_WIDTH, d)
    conv_p = (ssd_conv_w[layer], ssd_conv_b[layer], ssd_dt_bias[layer], ssd_a_log[layer])

    x2d = x.reshape(t, d)
    hc = _lnmod(ctx.reshape(bsz * ctx_len, d), norm_mix_w[layer], csh_m, csc_m, ctx_len, ctx_len)
    _, act_c, prep_c = _ssd_branch_inputs(hc, w_rest, *conv_p, ctx_len, fw, heads, _tile(bsz * ctx_len, 1024))
    h_zero = jnp.zeros((bsz, SSD_GROUPS, (heads // SSD_GROUPS) * SSD_HEAD_DIM, SSD_STATE), F32)
    _, hc_f = _ssd_scan(act_c, prep_c, h_zero, ctx_len, heads, False)
    _, hc_b = _ssd_scan(act_c, prep_c, h_zero, ctx_len, heads, True)

    tm = _tile(seq, 1024)
    tr = _tile(seq, 256)
    hx = _lnmod(x2d, norm_mix_w[layer], sh_m, sc_m, seq, tr)
    u = _mm_nt(hx, w_u, tm, 512, F32, "in_proj_fnet")
    f_out = _fourier_mix(u, fnet_w[layer], bsz, seq)
    rest, act, prep = _ssd_branch_inputs(hx, w_rest, *conv_p, seq, fw, heads, tm)
    y_f, _ = _ssd_scan(act, prep, hc_f, seq, heads, False)
    y_b, _ = _ssd_scan(act, prep, hc_b, seq, heads, True)
    d_exp = jnp.repeat(ssd_d[layer], SSD_HEAD_DIM)
    s_out = _ssd_out(y_f, y_b, act, rest, d_exp, ssd_norm_w[layer], tr)
    x1 = _wout(f_out, s_out, _cast_pad(w_out[layer], 0, d, d, d, FFN_TILE, d), x2d, g_m, seq, tm, 1024)

    hx2 = _lnmod(x1, norm_ffn_w[layer], sh_f, sc_f, seq, tr)
    f = ffn_w_gate.shape[2]
    fpad = -f % FFN_TILE
    w_gate = _cast_pad(ffn_w_gate[layer], 0, d, d, f + fpad, d, FFN_TILE)
    w_val = _cast_pad(ffn_w_val[layer], 0, d, d, f + fpad, d, FFN_TILE)
    w_down = _cast_pad(ffn_w_down[layer], 0, f, f + fpad, d, FFN_TILE, d)
    gate, val = _mm_pair(hx2, w_gate, w_val, tm, FFN_TILE, "ffn_up")
    conv_w9 = jnp.pad(ffn_conv_w[layer].reshape(9, f), ((0, 0), (0, fpad)))
    out = _ffn_down(gate, val, conv_w9, jnp.pad(ffn_conv_b[layer], (0, fpad)),
                    w_down, x1, g_f, norm_final_w, seq, _tile(seq, 512))
    return out.reshape(bsz, seq, d)
```

```python
import functools
import math

import numpy as np
import jax
import jax.numpy as jnp
from jax import lax
from jax.experimental import pallas as pl
from jax.experimental.pallas import tpu as pltpu

F32 = jnp.float32
BF16 = jnp.bfloat16
HIGHEST = lax.Precision.HIGHEST

GRID_W = 64
SSD_HEAD_DIM = 64
SSD_GROUPS = 8
SSD_STATE = 128
SSD_CHUNK = 128
EPS = 1e-6

LANES = 128
SUBLANES = 8
MXU_WIDTH = 256
IN_PROJ_TILE = 768
VMEM_LIMIT = 56 * 1024 * 1024
FFN_TILE = 512
FFN_DOWN_SLAB = 512


def _cparams(*sem):
    return pltpu.CompilerParams(dimension_semantics=sem, vmem_limit_bytes=VMEM_LIMIT)


def _silu(v):
    return v * jax.nn.sigmoid(v)


def _adaln_kernel(c_ref, w_ref, b_ref, o_ref):
    c = c_ref[...]
    s = _silu(c).astype(BF16)
    o_ref[...] = jnp.dot(s, w_ref[...].astype(BF16), preferred_element_type=F32) + b_ref[...]


def _adaln(cvecs, w, b):
    d, n = w.shape
    tn = 512
    return pl.pallas_call(
        _adaln_kernel,
        grid=(n // tn,),
        in_specs=[pl.BlockSpec((SUBLANES, d), lambda j: (0, 0)),
                  pl.BlockSpec((d, tn), lambda j: (0, j)),
                  pl.BlockSpec((1, tn), lambda j: (0, j))],
        out_specs=pl.BlockSpec((SUBLANES, tn), lambda j: (0, j)),
        out_shape=jax.ShapeDtypeStruct((SUBLANES, n), F32),
        compiler_params=_cparams("parallel"),
        name="adaln",
    )(cvecs, w, b.reshape(1, n))


def _lnmod_kernel(x_ref, w_ref, sh_ref, sc_ref, o_ref):
    x = x_ref[...]
    r = lax.rsqrt(jnp.mean(x * x, axis=-1, keepdims=True) + EPS)
    o_ref[...] = ((x * r) * w_ref[...] * (1.0 + sc_ref[0]) + sh_ref[0]).astype(o_ref.dtype)


def _lnmod(x2d, w, shift, scale, seq, tm):
    t, d = x2d.shape
    bsz = shift.shape[0]
    per = seq // tm
    return pl.pallas_call(
        _lnmod_kernel,
        grid=(t // tm,),
        in_specs=[pl.BlockSpec((tm, d), lambda i: (i, 0)),
                  pl.BlockSpec((1, d), lambda i: (0, 0)),
                  pl.BlockSpec((1, 1, d), lambda i: (i // per, 0, 0)),
                  pl.BlockSpec((1, 1, d), lambda i: (i // per, 0, 0))],
        out_specs=pl.BlockSpec((tm, d), lambda i: (i, 0)),
        out_shape=jax.ShapeDtypeStruct((t, d), BF16),
        compiler_params=_cparams("parallel"),
        name="lnmod",
    )(x2d, w.reshape(1, d), shift.reshape(bsz, 1, d), scale.reshape(bsz, 1, d))


def _cast_pad_kernel(w_ref, o_ref, *, rows, cols, tr, tc):
    r = lax.broadcasted_iota(jnp.int32, (tr, tc), 0) + pl.program_id(0) * tr
    c = lax.broadcasted_iota(jnp.int32, (tr, tc), 1) + pl.program_id(1) * tc
    o_ref[...] = jnp.where(jnp.logical_and(r < rows, c < cols), w_ref[...], 0.0).astype(o_ref.dtype)


def _cast_pad(w, row0, rows, out_rows, out_cols, tr, tc):
    cols = w.shape[1]
    assert row0 % tr == 0 and out_rows % tr == 0 and out_cols % tc == 0
    rb0 = row0 // tr
    return pl.pallas_call(
        functools.partial(_cast_pad_kernel, rows=rows, cols=cols, tr=tr, tc=tc),
        grid=(out_rows // tr, out_cols // tc),
        in_specs=[pl.BlockSpec((tr, tc), lambda i, j: (rb0 + i, j))],
        out_specs=pl.BlockSpec((tr, tc), lambda i, j: (i, j)),
        out_shape=jax.ShapeDtypeStruct((out_rows, out_cols), BF16),
        compiler_params=_cparams("parallel", "parallel"),
        name="weight_cast",
    )(w)


def _mm_nt_kernel(a_ref, wt_ref, o_ref):
    o_ref[...] = lax.dot_general(a_ref[...], wt_ref[...], (((1,), (1,)), ((), ())),
                                 preferred_element_type=F32).astype(o_ref.dtype)


def _mm_nt(a, wt, tm, tn, out_dtype, name):
    m, k = a.shape
    n = wt.shape[0]
    return pl.pallas_call(
        _mm_nt_kernel,
        grid=(m // tm, n // tn),
        in_specs=[pl.BlockSpec((tm, k), lambda i, j: (i, 0)),
                  pl.BlockSpec((tn, k), lambda i, j: (j, 0))],
        out_specs=pl.BlockSpec((tm, tn), lambda i, j: (i, j)),
        out_shape=jax.ShapeDtypeStruct((m, n), out_dtype),
        compiler_params=_cparams("parallel", "arbitrary"),
        name=name,
    )(a, wt)


def _mm_pair_kernel(a_ref, w1_ref, w2_ref, o1_ref, o2_ref):
    a = a_ref[...]
    o1_ref[0] = jnp.dot(a, w1_ref[...], preferred_element_type=F32).astype(o1_ref.dtype)
    o2_ref[0] = jnp.dot(a, w2_ref[...], preferred_element_type=F32).astype(o2_ref.dtype)


def _mm_pair(a, w1, w2, tm, tn, name):
    m, k = a.shape
    n = w1.shape[1]
    out = jax.ShapeDtypeStruct((n // tn, m, tn), BF16)
    return pl.pallas_call(
        _mm_pair_kernel,
        grid=(m // tm, n // tn),
        in_specs=[pl.BlockSpec((tm, k), lambda i, j: (i, 0)),
                  pl.BlockSpec((k, tn), lambda i, j: (0, j)),
                  pl.BlockSpec((k, tn), lambda i, j: (0, j))],
        out_specs=[pl.BlockSpec((1, tm, tn), lambda i, j: (j, i, 0)),
                   pl.BlockSpec((1, tm, tn), lambda i, j: (j, i, 0))],
        out_shape=[out, out],
        compiler_params=_cparams("parallel", "arbitrary"),
        name=name,
    )(a, w1, w2)


def _dft_tables(n):
    idx = np.arange(n)
    ang = 2.0 * np.pi * ((idx[:, None] * idx[None, :]) % n) / n
    return np.cos(ang), np.sin(ang)


def _fold_kernel(cs_ref, wf_ref, o_ref):
    o_ref[0] = jnp.dot(cs_ref[...], wf_ref[0], precision=HIGHEST,
                       preferred_element_type=F32).astype(o_ref.dtype)


def _fourier_fold(fnet_w, seq):
    g, c, _ = fnet_w.shape
    cc, sc = _dft_tables(c)
    cs = jnp.asarray(np.concatenate([cc, sc], axis=0) / math.sqrt(seq * c), F32)
    return pl.pallas_call(
        _fold_kernel,
        grid=(g,),
        in_specs=[pl.BlockSpec((2 * c, c), lambda i: (0, 0)),
                  pl.BlockSpec((1, c, c), lambda i: (i, 0, 0))],
        out_specs=pl.BlockSpec((1, 2 * c, c), lambda i: (i, 0, 0)),
        out_shape=jax.ShapeDtypeStruct((g, 2 * c, c), BF16),
        compiler_params=_cparams("parallel"),
        name="fnet_fold",
    )(cs, fnet_w)


def _dft1_kernel(f_ref, u_ref, zr_ref, zi_ref):
    n1 = zr_ref.shape[1]
    f = f_ref[...]
    for j in range(u_ref.shape[2]):
        z = jnp.dot(f, u_ref[0, :, j, :].astype(BF16), preferred_element_type=F32)
        zr_ref[0, :, j, :] = z[:n1]
        zi_ref[0, :, j, :] = z[n1:]


def _dft_stage1(u4):
    bsz, n1, n2, wdt = u4.shape
    c1, s1 = _dft_tables(n1)
    f1 = jnp.asarray(np.concatenate([c1, -s1], axis=0), BF16)
    spec = pl.BlockSpec((1, n1, SUBLANES, wdt), lambda b, j: (b, 0, j, 0))
    return pl.pallas_call(
        _dft1_kernel,
        grid=(bsz, n2 // SUBLANES),
        in_specs=[pl.BlockSpec((2 * n1, n1), lambda b, j: (0, 0)), spec],
        out_specs=[spec, spec],
        out_shape=[jax.ShapeDtypeStruct(u4.shape, F32)] * 2,
        compiler_params=_cparams("parallel", "parallel"),
        name="fnet_dft_stage1",
    )(f1, u4)


def _dft2_kernel(c2_ref, s2_ref, tc_ref, ts_ref, wcs_ref, zr_ref, zi_ref, o_ref, *, kb, groups, cg):
    c2 = c2_ref[...]
    s2 = s2_ref[...]
    for j in range(kb):
        tcj = tc_ref[j:j + 1, :]
        tsj = ts_ref[j:j + 1, :]
        gc = c2 * tcj - s2 * tsj
        gs = s2 * tcj + c2 * tsj
        gmat = jnp.concatenate([jnp.concatenate([gc, gs], axis=1),
                                jnp.concatenate([-gs, gc], axis=1)], axis=0).astype(BF16)
        z = jnp.concatenate([zr_ref[0, j], zi_ref[0, j]], axis=0).astype(BF16)
        p = jnp.dot(gmat, z, preferred_element_type=F32)
        n2 = c2.shape[0]
        pr = p[:n2].astype(BF16)
        pi = p[n2:].astype(BF16)
        for g in range(groups):
            sl = slice(g * cg, (g + 1) * cg)
            y = jnp.dot(jnp.concatenate([pr[:, sl], pi[:, sl]], axis=1), wcs_ref[g],
                        preferred_element_type=F32)
            o_ref[0, :, j, sl] = y.astype(o_ref.dtype)


def _dft_stage2(zr4, zi4, wcs, seq):
    bsz, n1, n2, wdt = zr4.shape
    groups, _, cg = wcs.shape
    c2, s2 = _dft_tables(n2)
    k1 = np.arange(n1)[:, None]
    l2 = np.arange(n2)[None, :]
    ang = 2.0 * np.pi * ((k1 * l2) % seq) / seq
    kb = SUBLANES
    zspec = pl.BlockSpec((1, kb, n2, wdt), lambda b, i: (b, i, 0, 0))
    full = lambda shape: pl.BlockSpec(shape, lambda b, i: (0,) * len(shape))
    return pl.pallas_call(
        functools.partial(_dft2_kernel, kb=kb, groups=groups, cg=cg),
        grid=(bsz, n1 // kb),
        in_specs=[full((n2, n2)), full((n2, n2)),
                  pl.BlockSpec((kb, n2), lambda b, i: (i, 0)),
                  pl.BlockSpec((kb, n2), lambda b, i: (i, 0)),
                  full(wcs.shape), zspec, zspec],
        out_specs=pl.BlockSpec((1, n2, kb, wdt), lambda b, i: (b, 0, i, 0)),
        out_shape=jax.ShapeDtypeStruct((bsz, n2, n1, wdt), BF16),
        compiler_params=_cparams("parallel", "parallel"),
        name="fnet_dft_stage2",
    )(jnp.asarray(c2, F32), jnp.asarray(s2, F32), jnp.asarray(np.cos(ang), F32),
      jnp.asarray(np.sin(ang), F32), wcs, zr4, zi4)


def _fourier_mix(u, fnet_w, bsz, seq):
    wdt = u.shape[1]
    n2 = LANES
    n1 = seq // n2
    wcs = _fourier_fold(fnet_w, seq)
    zr, zi = _dft_stage1(u.reshape(bsz, n1, n2, wdt))
    out = _dft_stage2(zr, zi, wcs, seq)
    return out.reshape(bsz * seq, wdt)


def _dwconv_silu_kernel(prev_ref, main_ref, next_ref, w_ref, b_ref, o_ref, *, per, tl, taps):
    t = pl.program_id(0) % per
    h = SUBLANES
    half = (taps - 1) // 2
    x = main_ref[...]
    acc = b_ref[...] + w_ref[half:half + 1, :] * x
    for k in range(taps):
        if k != half:
            acc = acc + w_ref[k:k + 1, :] * pltpu.roll(x, (half - k) % tl, 0)
    o_ref[...] = _silu(acc).astype(o_ref.dtype)

    def edge(win):
        e = b_ref[...] + w_ref[0:1, :] * win[h - half:2 * h - half]
        for k in range(1, taps):
            e = e + w_ref[k:k + 1, :] * win[h - half + k:2 * h - half + k]
        return _silu(e).astype(o_ref.dtype)

    top = jnp.where(t == 0, 0.0, prev_ref[...])
    o_ref[0:h, :] = edge(jnp.concatenate([top, x[0:2 * h]], axis=0))
    bot = jnp.where(t == per - 1, 0.0, next_ref[...])
    o_ref[tl - h:tl, :] = edge(jnp.concatenate([x[tl - 2 * h:tl], bot], axis=0))


def _dwconv_silu(src, col0, width, conv_w, conv_b, seq, tl, tc):
    t = src.shape[0]
    taps = conv_w.shape[0]
    per = seq // tl
    cb0 = col0 // tc
    hb = tl // SUBLANES
    last = t // SUBLANES - 1
    return pl.pallas_call(
        functools.partial(_dwconv_silu_kernel, per=per, tl=tl, taps=taps),
        grid=(t // tl, width // tc),
        in_specs=[pl.BlockSpec((SUBLANES, tc), lambda i, j: (jnp.maximum(i * hb - 1, 0), cb0 + j)),
                  pl.BlockSpec((tl, tc), lambda i, j: (i, cb0 + j)),
                  pl.BlockSpec((SUBLANES, tc), lambda i, j: (jnp.minimum((i + 1) * hb, last), cb0 + j)),
                  pl.BlockSpec((taps, tc), lambda i, j: (0, j)),
                  pl.BlockSpec((1, tc), lambda i, j: (0, j))],
        out_specs=pl.BlockSpec((tl, tc), lambda i, j: (i, j)),
        out_shape=jax.ShapeDtypeStruct((t, width), BF16),
        compiler_params=_cparams("parallel", "parallel"),
        name="ssd_dwconv_silu",
    )(src, src, src, conv_w, conv_b.reshape(1, width))


def _ssd_prep_kernel(raw_ref, bias_ref, alog_ref, dtT_ref, acol_ref, aT_ref, w_ref, *, heads):
    c = SSD_CHUNK
    lane = lax.broadcasted_iota(jnp.int32, (c, LANES), 1)
    lane_row = lax.broadcasted_iota(jnp.int32, (1, LANES), 1)
    li = lax.broadcasted_iota(jnp.int32, (c, c), 0)
    si = lax.broadcasted_iota(jnp.int32, (c, c), 1)
    lower = (si <= li).astype(F32)
    upper = (si >= li).astype(F32)
    a_neg = -jnp.exp(alog_ref[...])
    for i in range(raw_ref.shape[0] // c):
        rows = slice(i * c, (i + 1) * c)
        v = raw_ref[rows, :] + bias_ref[...]
        dt = jnp.maximum(v, 0.0) + jnp.log1p(jnp.exp(-jnp.abs(v)))
        dt = jnp.where(lane < 2 * heads, dt, 0.0)
        dta = dt * a_neg
        pre = jnp.dot(lower, dta, precision=HIGHEST, preferred_element_type=F32)
        suf = jnp.dot(upper, dta, precision=HIGHEST, preferred_element_type=F32)
        acum = jnp.where(lane < heads, pre, suf)
        aend = jnp.where(lane_row < heads, acum[c - 1:c, :], acum[0:1, :])
        dtT_ref[rows, :] = dt.T
        acol_ref[rows, :] = acum
        aT_ref[rows, :] = acum.T
        w_ref[rows, :] = (dt * jnp.exp(aend - acum)).T


def _ssd_prep(src, colblk, dt_bias, a_log, heads):
    t = src.shape[0]
    c = SSD_CHUNK
    pad = LANES - 2 * heads
    bias = jnp.pad(dt_bias.reshape(1, 2 * heads), ((0, 0), (0, pad)))
    alog = jnp.pad(a_log.reshape(1, 2 * heads), ((0, 0), (0, pad)))
    rows = _tile(t, 4 * c)
    spec = pl.BlockSpec((rows, LANES), lambda i: (i, 0))
    vec = pl.BlockSpec((1, LANES), lambda i: (0, 0))
    return pl.pallas_call(
        functools.partial(_ssd_prep_kernel, heads=heads),
        grid=(t // rows,),
        in_specs=[pl.BlockSpec((rows, LANES), lambda i: (i, colblk)), vec, vec],
        out_specs=[spec] * 4,
        out_shape=[jax.ShapeDtypeStruct((t, LANES), F32)] * 4,
        compiler_params=_cparams("parallel"),
        name="ssd_prep",
    )(src, bias, alog)


def _ssd_scan_kernel(x_ref, b_ref, c_ref, dtT_ref, acol_ref, aT_ref, w_ref, h0_ref, y_ref, hout_ref, h_scr,
                     *, reverse, col0, groups, hpg, nsteps):
    c = SSD_CHUNK
    n = SSD_STATE
    p = SSD_HEAD_DIM
    bsz = x_ref.shape[0]
    s = pl.program_id(0)

    @pl.when(s == 0)
    def _():
        h_scr[...] = h0_ref[...]

    li = lax.broadcasted_iota(jnp.int32, (c, c), 0)
    si = lax.broadcasted_iota(jnp.int32, (c, c), 1)
    mask = (li <= si) if reverse else (li >= si)
    lo_half = lax.broadcasted_iota(jnp.int32, (c, LANES), 1) < p
    pairs = hpg // 2
    nt = (((1,), (1,)), ((), ()))

    def stream(b):
        acol = acol_ref[b]
        row = lambda ref, k: ref[b, k:k + 1, :]
        eend = jnp.exp(acol[0:1, :] if reverse else acol[c - 1:c, :])
        shared = {}

        def group_piece(g):
            bg = b_ref[b, :, g * n:(g + 1) * n]
            cg = c_ref[b, :, g * n:(g + 1) * n]
            h = h_scr[b, g]
            shared[g] = (bg,
                         lax.dot_general(cg, bg, nt, preferred_element_type=F32),
                         h,
                         lax.dot_general(cg, h.astype(BF16), nt, preferred_element_type=F32))

        def pair_piece(g, pr):
            bg, cb, h, yoff = shared[g]
            lanes = slice((g * pairs + pr) * LANES, (g * pairs + pr + 1) * LANES)
            loc = slice(pr * LANES, (pr + 1) * LANES)
            xpb = x_ref[b, :, lanes]
            k0 = col0 + g * hpg + 2 * pr
            ys, abc = [], []
            for q in range(2):
                k = k0 + q
                abc.append(jnp.broadcast_to(acol[:, k:k + 1], (c, c)))
                dec = jnp.exp(jnp.where(mask, abc[q] - row(aT_ref, k), -jnp.inf))
                m = (cb * dec * row(dtT_ref, k)).astype(BF16)
                ys.append(jnp.dot(m, xpb, preferred_element_type=F32))
            e_pair = jnp.exp(jnp.where(lo_half, abc[0], abc[1]))
            y = jnp.where(lo_half, ys[0], ys[1]) + yoff[:, loc] * e_pair
            y_ref[b, :, lanes] = y.astype(y_ref.dtype)
            per_head = lambda v0, v1, width: jnp.concatenate(
                [jnp.broadcast_to(v0, (p, width)), jnp.broadcast_to(v1, (p, width))], axis=0)
            xw_t = (xpb.astype(F32).T * per_head(row(w_ref, k0), row(w_ref, k0 + 1), c)).astype(BF16)
            upd = jnp.dot(xw_t, bg, preferred_element_type=F32)
            h_scr[b, g, loc, :] = h[loc] * per_head(eend[:, k0:k0 + 1], eend[:, k0 + 1:k0 + 2], n) + upd

        pieces = []
        for g in range(groups):
            pieces.append(functools.partial(group_piece, g))
            pieces.extend(functools.partial(pair_piece, g, pr) for pr in range(pairs))
        return pieces

    for step in zip(*[stream(b) for b in range(bsz)]):
        for piece in step:
            piece()

    @pl.when(s == nsteps - 1)
    def _():
        hout_ref[...] = h_scr[...]


def _ssd_scan(act, prep, h0, seq, heads, reverse):
    t = act.shape[0]
    bsz = t // seq
    c, n, groups = SSD_CHUNK, SSD_STATE, SSD_GROUPS
    hpg = heads // groups
    width = heads * SSD_HEAD_DIM
    gn = groups * n
    nsteps = seq // c
    act3 = act.reshape(bsz, seq, act.shape[1])
    prep3 = [p.reshape(bsz, seq, LANES) for p in prep]

    def chunk(s):
        return (nsteps - 1 - s) if reverse else s

    small = pl.BlockSpec((bsz, c, LANES), lambda s: (0, chunk(s), 0))
    hspec = pl.BlockSpec(h0.shape, lambda s: (0, 0, 0, 0))
    y, h = pl.pallas_call(
        functools.partial(_ssd_scan_kernel, reverse=reverse, col0=heads if reverse else 0,
                          groups=groups, hpg=hpg, nsteps=nsteps),
        grid=(nsteps,),
        in_specs=[pl.BlockSpec((bsz, c, width), lambda s: (0, chunk(s), 0)),
                  pl.BlockSpec((bsz, c, gn), lambda s: (0, chunk(s), width // gn)),
                  pl.BlockSpec((bsz, c, gn), lambda s: (0, chunk(s), width // gn + 1)),
                  small, small, small, small, hspec],
        out_specs=[pl.BlockSpec((bsz, c, width), lambda s: (0, chunk(s), 0)), hspec],
        out_shape=[jax.ShapeDtypeStruct((bsz, seq, width), BF16),
                   jax.ShapeDtypeStruct(h0.shape, F32)],
        scratch_shapes=[pltpu.VMEM(h0.shape, F32)],
        compiler_params=_cparams("arbitrary"),
        name="ssd_scan_bwd" if reverse else "ssd_scan_fwd",
    )(act3, act3, act3, *prep3, h0)
    return y.reshape(t, width), h


def _ssd_out_kernel(yf_ref, yb_ref, x_ref, z_ref, d_ref, nw_ref, o_ref, *, groups, gw):
    y = yf_ref[...].astype(F32) + yb_ref[...].astype(F32) + d_ref[...] * x_ref[...].astype(F32)
    g = y * _silu(z_ref[...])
    for i in range(groups):
        sl = slice(i * gw, (i + 1) * gw)
        gi = g[:, sl]
        r = lax.rsqrt(jnp.mean(gi * gi, axis=-1, keepdims=True) + EPS)
        o_ref[:, sl] = (gi * r * nw_ref[:, sl]).astype(o_ref.dtype)


def _ssd_out(yf, yb, act, rest, d_exp, norm_w, tm):
    t, width = yf.shape
    gw = width // SSD_GROUPS
    row = pl.BlockSpec((tm, width), lambda i: (i, 0))
    vec = pl.BlockSpec((1, width), lambda i: (0, 0))
    return pl.pallas_call(
        functools.partial(_ssd_out_kernel, groups=SSD_GROUPS, gw=gw),
        grid=(t // tm,),
        in_specs=[row, row, row, row, vec, vec],
        out_specs=row,
        out_shape=jax.ShapeDtypeStruct((t, width), BF16),
        compiler_params=_cparams("parallel"),
        name="ssd_out",
    )(yf, yb, act, rest, d_exp.reshape(1, width), norm_w.reshape(1, width))


def _wout_kernel(f_ref, s_ref, w_ref, x_ref, g_ref, o_ref):
    k1 = f_ref.shape[1]
    acc = jnp.dot(f_ref[...], w_ref[:k1, :], preferred_element_type=F32)
    acc = acc + jnp.dot(s_ref[...], w_ref[k1:, :], preferred_element_type=F32)
    o_ref[...] = x_ref[...] + g_ref[0] * acc


def _wout(f_out, s_out, w, x2d, gate, seq, tm, tn):
    t, d = x2d.shape
    bsz = gate.shape[0]
    k1, k2 = f_out.shape[1], s_out.shape[1]
    per = seq // tm
    return pl.pallas_call(
        _wout_kernel,
        grid=(t // tm, d // tn),
        in_specs=[pl.BlockSpec((tm, k1), lambda i, j: (i, 0)),
                  pl.BlockSpec((tm, k2), lambda i, j: (i, 0)),
                  pl.BlockSpec((k1 + k2, tn), lambda i, j: (0, j)),
                  pl.BlockSpec((tm, tn), lambda i, j: (i, j)),
                  pl.BlockSpec((1, 1, tn), lambda i, j: (i // per, 0, j))],
        out_specs=pl.BlockSpec((tm, tn), lambda i, j: (i, j)),
        out_shape=jax.ShapeDtypeStruct((t, d), F32),
        compiler_params=_cparams("parallel", "arbitrary"),
        name="out_proj",
    )(f_out, s_out, w, x2d, gate.reshape(bsz, 1, d))


def _ffn_down_kernel(up_ref, main_ref, dn_ref, val_ref, cw_ref, cb_ref, w_ref, x_ref, g_ref, nw_ref, o_ref,
                     a0_scr, a1_scr, *, per, nk, nsteps):
    q = pl.program_id(0)
    qc = jnp.minimum(q, nsteps - 1)
    t = (qc // nk) % per
    kc = qc % nk
    km = jnp.maximum(q - 1, 0) % nk
    gw = GRID_W
    tm, tk = a0_scr.shape

    @pl.when(q == 0)
    def _():
        a1_scr[...] = jnp.zeros((tm, tk), BF16)

    @pl.when(km == 0)
    def _():
        o_ref[...] = jnp.zeros_like(o_ref)

    def conv_and_matmul(a_new, a_prev):
        col = lax.broadcasted_iota(jnp.int32, (gw, LANES), 0)
        not_first = col != 0
        not_last = col != gw - 1
        nrows = tm // gw

        def conv_piece(c, r):
            ls = slice(c * LANES, (c + 1) * LANES)
            cw = cw_ref[kc, :, ls]
            rows = slice(r * gw, (r + 1) * gw)
            if r == 0:
                above = jnp.where(t == 0, 0.0, up_ref[0, :, ls].astype(F32))
            else:
                above = main_ref[0, (r - 1) * gw:r * gw, ls].astype(F32)
            mid = main_ref[0, rows, ls].astype(F32)
            if r == nrows - 1:
                below = jnp.where(t == per - 1, 0.0, dn_ref[0, :, ls].astype(F32))
            else:
                below = main_ref[0, (r + 1) * gw:(r + 2) * gw, ls].astype(F32)
            p = [cw[dx:dx + 1] * above + cw[3 + dx:4 + dx] * mid + cw[6 + dx:7 + dx] * below
                 for dx in range(3)]
            left = jnp.where(not_first, pltpu.roll(p[0], 1, 0), 0.0)
            right = jnp.where(not_last, pltpu.roll(p[2], gw - 1, 0), 0.0)
            acc = p[1] + cb_ref[kc, :, ls] + left + right
            a_new[rows, ls] = (_silu(acc) * val_ref[0, rows, ls].astype(F32)).astype(BF16)

        def matmul_piece(n):
            ns = slice(n * FFN_DOWN_SLAB, (n + 1) * FFN_DOWN_SLAB)
            o_ref[:, ns] += jnp.dot(a_prev[...], w_ref[:, ns], preferred_element_type=F32)

        pieces = [(c, r) for c in range(tk // LANES) for r in range(nrows)]
        nslabs = o_ref.shape[1] // FFN_DOWN_SLAB
        per_slab = -(-len(pieces) // nslabs)
        for n in range(nslabs):
            matmul_piece(n)
            for c, r in pieces[n * per_slab:(n + 1) * per_slab]:
                conv_piece(c, r)

    conv_and_matmul(a0_scr, a1_scr)

    @pl.when(q < nsteps)
    def _():
        a1_scr[...] = a0_scr[...]

    @pl.when(jnp.logical_and(km == nk - 1, q > 0))
    def _():
        for r in range(tm // SUBLANES):
            rows = slice(r * SUBLANES, (r + 1) * SUBLANES)
            v = x_ref[rows, :] + g_ref[0] * o_ref[rows, :]
            s = lax.rsqrt(jnp.mean(v * v, axis=-1, keepdims=True) + EPS)
            o_ref[rows, :] = v * s * nw_ref[...]


def _ffn_down(gate, val, conv_w9, conv_b, w, x2d, g, norm_w, seq, tm):
    nk, t, tk = gate.shape
    d = x2d.shape[1]
    bsz = g.shape[0]
    per = seq // tm
    hb = tm // GRID_W
    last = t // GRID_W - 1
    nsteps = (t // tm) * nk
    conv_w3 = conv_w9.reshape(9, nk, tk).swapaxes(0, 1)
    conv_b3 = conv_b.reshape(nk, 1, tk)

    def conv_tile(q):
        qc = jnp.minimum(q, nsteps - 1)
        return qc // nk, qc % nk

    def mm_tile(q):
        qm = jnp.maximum(q - 1, 0)
        return qm // nk, qm % nk

    def main_map(q):
        i, k = conv_tile(q)
        return k, i, 0

    def up_map(q):
        i, k = conv_tile(q)
        return k, jnp.maximum(i * hb - 1, 0), 0

    def dn_map(q):
        i, k = conv_tile(q)
        return k, jnp.minimum((i + 1) * hb, last), 0

    return pl.pallas_call(
        functools.partial(_ffn_down_kernel, per=per, nk=nk, nsteps=nsteps),
        grid=(nsteps + 1,),
        in_specs=[pl.BlockSpec((1, GRID_W, tk), up_map),
                  pl.BlockSpec((1, tm, tk), main_map),
                  pl.BlockSpec((1, GRID_W, tk), dn_map),
                  pl.BlockSpec((1, tm, tk), main_map),
                  pl.BlockSpec((nk, 9, tk), lambda q: (0, 0, 0)),
                  pl.BlockSpec((nk, 1, tk), lambda q: (0, 0, 0)),
                  pl.BlockSpec((tk, d), lambda q: (mm_tile(q)[1], 0)),
                  pl.BlockSpec((tm, d), lambda q: (mm_tile(q)[0], 0)),
                  pl.BlockSpec((1, 1, d), lambda q: (mm_tile(q)[0] // per, 0, 0)),
                  pl.BlockSpec((1, d), lambda q: (0, 0))],
        out_specs=pl.BlockSpec((tm, d), lambda q: (mm_tile(q)[0], 0)),
        out_shape=jax.ShapeDtypeStruct((t, d), F32),
        scratch_shapes=[pltpu.VMEM((tm, tk), BF16), pltpu.VMEM((tm, tk), BF16)],
        compiler_params=_cparams("arbitrary"),
        name="ffn_conv_down",
    )(gate, gate, gate, val, conv_w3, conv_b3, w, x2d, g.reshape(bsz, 1, d), norm_w.reshape(1, d))


def _tile(n, pref):
    t = pref
    while n % t:
        t //= 2
    return t


def _ssd_branch_inputs(h, w_rest, conv_w, conv_b, dt_bias, a_log, seq, fw, heads, tm):
    width = heads * SSD_HEAD_DIM
    conv_dim = conv_w.shape[1]
    rest = _mm_nt(h, w_rest, tm, IN_PROJ_TILE, F32, "in_proj_ssd")
    act = _dwconv_silu(rest, width, conv_dim, conv_w, conv_b, seq, _tile(seq, 512), 512)
    prep = _ssd_prep(rest, (width + conv_dim) // LANES, dt_bias, a_log, heads)
    return rest, act, prep


def kernel(x, c, ctx, c_ctx, ada_w, ada_b, norm_mix_w, w_in, fnet_w, ssd_conv_w, ssd_conv_b, ssd_dt_bias,
           ssd_a_log, ssd_d, ssd_norm_w, w_out, norm_ffn_w, ffn_w_gate, ffn_w_val, ffn_conv_w, ffn_conv_b,
           ffn_w_down, norm_final_w):
    bsz, seq, d = x.shape
    ctx_len = ctx.shape[1]
    depth = ada_w.shape[0]
    fw = fnet_w.shape[1] * fnet_w.shape[2]
    width = d - fw
    heads = width // SSD_HEAD_DIM
    conv_dim = ssd_conv_w.shape[2]
    assert depth == 1, "only the single-layer block is implemented"
    assert 2 * heads <= LANES and heads % SSD_GROUPS == 0 and (heads // SSD_GROUPS) % 2 == 0
    t = bsz * seq
    layer = 0

    cvecs = jnp.zeros((SUBLANES, d), F32).at[:bsz].set(c).at[bsz].set(c_ctx)
    mod = _adaln(cvecs, ada_w[layer], ada_b[layer])
    sh_m, sc_m, g_m, sh_f, sc_f, g_f = [mod[:bsz, i * d:(i + 1) * d] for i in range(6)]
    csh_m, csc_m = [jnp.broadcast_to(mod[bsz:bsz + 1, i * d:(i + 1) * d], (bsz, d)) for i in range(2)]

    w_in_t = jnp.swapaxes(w_in[layer], 0, 1)
    n_rest = w_in_t.shape[0] - fw
    w_u = _cast_pad(w_in_t, 0, fw, fw, d, MXU_WIDTH, d)
    w_rest = _cast_pad(w_in_t, fw, n_rest, n_rest + (-n_rest % IN_PROJ_TILE), d, MXU_WIDTH, d)
    conv_p = (ssd_conv_w[layer], ssd_conv_b[layer], ssd_dt_bias[layer], ssd_a_log[layer])

    x2d = x.reshape(t, d)
    hc = _lnmod(ctx.reshape(bsz * ctx_len, d), norm_mix_w[layer], csh_m, csc_m, ctx_len, ctx_len)
    _, act_c, prep_c = _ssd_branch_inputs(hc, w_rest, *conv_p, ctx_len, fw, heads, _tile(bsz * ctx_len, 1024))
    h_zero = jnp.zeros((bsz, SSD_GROUPS, (heads // SSD_GROUPS) * SSD_HEAD_DIM, SSD_STATE), F32)
    _, hc_f = _ssd_scan(act_c, prep_c, h_zero, ctx_len, heads, False)
    _, hc_b = _ssd_scan(act_c, prep_c, h_zero, ctx_len, heads, True)

    tm = _tile(seq, 1024)
    tr = _tile(seq, 256)
    hx = _lnmod(x2d, norm_mix_w[layer], sh_m, sc_m, seq, 2 * tr)
    u = _mm_nt(hx, w_u, tm, 512, F32, "in_proj_fnet")
    f_out = _fourier_mix(u, fnet_w[layer], bsz, seq)
    rest, act, prep = _ssd_branch_inputs(hx, w_rest, *conv_p, seq, fw, heads, tm)
    y_f, _ = _ssd_scan(act, prep, hc_f, seq, heads, False)
    y_b, _ = _ssd_scan(act, prep, hc_b, seq, heads, True)
    d_exp = jnp.repeat(ssd_d[layer], SSD_HEAD_DIM)
    s_out = _ssd_out(y_f, y_b, act, rest, d_exp, ssd_norm_w[layer], tr)
    x1 = _wout(f_out, s_out, _cast_pad(w_out[layer], 0, d, d, d, FFN_TILE, d), x2d, g_m, seq, tm, 1024)

    hx2 = _lnmod(x1, norm_ffn_w[layer], sh_f, sc_f, seq, 2 * tr)
    f = ffn_w_gate.shape[2]
    fpad = -f % FFN_TILE
    w_gate = _cast_pad(ffn_w_gate[layer], 0, d, d, f + fpad, d, FFN_TILE)
    w_val = _cast_pad(ffn_w_val[layer], 0, d, d, f + fpad, d, FFN_TILE)
    w_down = _cast_pad(ffn_w_down[layer], 0, f, f + fpad, d, FFN_TILE, d)
    gate, val = _mm_pair(hx2, w_gate, w_val, tm, FFN_TILE, "ffn_up")
    conv_w9 = jnp.pad(ffn_conv_w[layer].reshape(9, f), ((0, 0), (0, fpad)))
    out = _ffn_down(gate, val, conv_w9, jnp.pad(ffn_conv_b[layer], (0, fpad)),
                    w_down, x1, g_f, norm_final_w, seq, _tile(seq, 512))
    return out.reshape(bsz, seq, d)
```
